```python
import math
import jax
import jax.numpy as jnp
from jax import lax
import numpy as np


D_MODEL = 1024
BATCH = 4
SEQ = 4096
DEPTH = 2
DEC_BATCH = 128
DEC_SEQ = 4
PAST_LEN = 2048
PAGE_SIZE = 128

N_MIXERS = 2
N_GDN_LAYERS = (DEPTH + 1) // 2
N_FOX_LAYERS = DEPTH // 2
N_META = 16
EPS = 1e-6

GDN_HEADS = D_MODEL // 128
GDN_HEAD_DIM = 128
GDN_HD = GDN_HEADS * GDN_HEAD_DIM
GDN_IN = 4 * GDN_HD + 2 * GDN_HEADS
CONV_W = 4
GDN_CHUNK = 64
GDN_SCALE = GDN_HEAD_DIM ** -0.5

FOX_HEADS = D_MODEL // 128
FOX_HEAD_DIM = 128
FOX_HD = FOX_HEADS * FOX_HEAD_DIM
FOX_IN = 3 * FOX_HD + FOX_HEADS
Q_BLOCK = 128
FOX_SCALE = FOX_HEAD_DIM ** -0.5

PEER_HEADS = 8
PEER_TOPK = 16
N_KEYS = 128
N_EXPERTS = N_KEYS * N_KEYS
PEER_HALF = 128
PEER_QUERY_DIM = 2 * PEER_HALF
PEER_BLOCK = 256

kernel_name = 'hybrid_gdn_fox_peer_meta_step'


def rms_norm(x, g):
    xf = x.astype(jnp.float32)
    y = xf * lax.rsqrt(jnp.mean(xf * xf, axis=-1, keepdims=True) + EPS)
    return (y * g.astype(jnp.float32)).astype(x.dtype)


def l2norm(x):
    return x * lax.rsqrt(jnp.sum(x * x, axis=-1, keepdims=True) + EPS)


def short_conv(x_pad, w):
    t = x_pad.shape[1] - (CONV_W - 1)
    y = x_pad[:, 0:t] * w[0]
    for i in range(1, CONV_W):
        y = y + x_pad[:, i:i + t] * w[i]
    return jax.nn.silu(y)


def gdn_split(z):
    qkv = z[..., :3 * GDN_HD]
    gate = z[..., 3 * GDN_HD:4 * GDN_HD]
    a = z[..., 4 * GDN_HD:4 * GDN_HD + GDN_HEADS]
    b = z[..., 4 * GDN_HD + GDN_HEADS:]
    return qkv, gate, a, b


def gdn_heads(qkv_c, a, b, a_log, dt_bias):
    bs, t, _ = qkv_c.shape
    q, k, v = jnp.split(qkv_c.astype(jnp.float32), 3, axis=-1)
    q = l2norm(q.reshape(bs, t, GDN_HEADS, GDN_HEAD_DIM)) * GDN_SCALE
    k = l2norm(k.reshape(bs, t, GDN_HEADS, GDN_HEAD_DIM))
    v = v.reshape(bs, t, GDN_HEADS, GDN_HEAD_DIM)
    g = -jnp.exp(a_log.astype(jnp.float32)) * jax.nn.softplus(a.astype(jnp.float32) + dt_bias.astype(jnp.float32))
    beta = jax.nn.sigmoid(b.astype(jnp.float32))
    sw = lambda x: jnp.swapaxes(x, 1, 2)
    return sw(q), sw(k), sw(v), sw(beta), sw(g)


def gdn_chunk(s, q, k, v, beta, g):
    ln = q.shape[2]
    causal = jnp.tril(jnp.ones((ln, ln), dtype=bool))
    strict = jnp.tril(jnp.ones((ln, ln), dtype=bool), -1)
    gc = jnp.cumsum(g, axis=-1)
    decay = jnp.exp(jnp.where(causal, gc[..., :, None] - gc[..., None, :], -jnp.inf))
    k_beta = k * beta[..., None]
    a_mat = jnp.where(strict, jnp.einsum('bhik,bhjk->bhij', k_beta, k) * decay, 0.0)
    eye = jnp.eye(ln, dtype=q.dtype)
    rhs = jnp.concatenate([v * beta[..., None], k_beta * jnp.exp(gc)[..., None]], axis=-1)
    sol = lax.linalg.triangular_solve(a_mat + eye, rhs, left_side=True, lower=True, unit_diagonal=True)
    u_c, w_c = sol[..., :GDN_HEAD_DIM], sol[..., GDN_HEAD_DIM:]
    v_new = u_c - jnp.einsum('bhlk,bhkv->bhlv', w_c, s)
    attn = jnp.einsum('bhik,bhjk->bhij', q, k) * decay
    o = jnp.einsum('bhlk,bhkv->bhlv', q * jnp.exp(gc)[..., None], s) + jnp.einsum('bhij,bhjv->bhiv', attn, v_new)
    g_last = gc[..., -1]
    s_new = s * jnp.exp(g_last)[..., None, None] + jnp.einsum(
        'bhlk,bhlv->bhkv', k * jnp.exp(g_last[..., None] - gc)[..., None], v_new)
    return s_new, o


def to_chunks(x):
    bs, h, t = x.shape[:3]
    x = x.reshape((bs, h, t // GDN_CHUNK, GDN_CHUNK) + x.shape[3:])
    return jnp.moveaxis(x, 2, 0)


def gdn_out(h, o, gate, out_norm, w_out):
    bs, t, _ = h.shape
    o = jnp.swapaxes(o, 1, 2)
    o = rms_norm(o, out_norm) * jax.nn.silu(gate.reshape(bs, t, GDN_HEADS, GDN_HEAD_DIM).astype(jnp.float32))
    return o.reshape(bs, t, GDN_HD).astype(h.dtype) @ w_out


def gdn_prompt(h, w_in, conv_w, a_log, dt_bias, out_norm, w_out):
    bs, t, _ = h.shape
    qkv, gate, a, b = gdn_split(h @ w_in)
    qkv_pad = jnp.pad(qkv, ((0, 0), (CONV_W - 1, 0), (0, 0)))
    new_conv = qkv_pad[:, -(CONV_W - 1):]
    q, k, v, beta, g = gdn_heads(short_conv(qkv_pad, conv_w.astype(qkv.dtype)), a, b, a_log, dt_bias)
    s0 = jnp.zeros((bs, GDN_HEADS, GDN_HEAD_DIM, GDN_HEAD_DIM), jnp.float32)
    s1, o_meta = gdn_chunk(s0, q[:, :, :N_META], k[:, :, :N_META], v[:, :, :N_META],
                           beta[:, :, :N_META], g[:, :, :N_META])
    xs = (to_chunks(q[:, :, N_META:]), to_chunks(k[:, :, N_META:]), to_chunks(v[:, :, N_META:]),
          to_chunks(beta[:, :, N_META:]), to_chunks(g[:, :, N_META:]))
    s_fin, o_rest = lax.scan(lambda st, c: gdn_chunk(st, c[0], c[1], c[2], c[3], c[4]), s1, xs)
    o_rest = jnp.moveaxis(o_rest, 0, 2).reshape(bs, GDN_HEADS, t - N_META, GDN_HEAD_DIM)
    o = jnp.concatenate([o_meta, o_rest], axis=2)
    return gdn_out(h, o, gate, out_norm, w_out), new_conv, s_fin


def gdn_sample(h, conv_state, ssm_state, w_in, conv_w, a_log, dt_bias, out_norm, w_out):
    qkv, gate, a, b = gdn_split(h @ w_in)
    qkv_pad = jnp.concatenate([conv_state.astype(qkv.dtype), qkv], axis=1)
    new_conv = qkv_pad[:, -(CONV_W - 1):]
    q, k, v, beta, g = gdn_heads(short_conv(qkv_pad, conv_w.astype(qkv.dtype)), a, b, a_log, dt_bias)
    s_fin, o = gdn_chunk(ssm_state.astype(jnp.float32), q, k, v, beta, g)
    return gdn_out(h, o, gate, out_norm, w_out), new_conv, s_fin


def fox_project(h, w_in, b_f):
    bs, t, _ = h.shape
    z = h @ w_in
    q = z[..., :FOX_HD].reshape(bs, t, FOX_HEADS, FOX_HEAD_DIM)
    k = z[..., FOX_HD:2 * FOX_HD].reshape(bs, t, FOX_HEADS, FOX_HEAD_DIM)
    v = z[..., 2 * FOX_HD:3 * FOX_HD].reshape(bs, t, FOX_HEADS, FOX_HEAD_DIM)
    logf = jax.nn.log_sigmoid((z[..., 3 * FOX_HD:] + b_f).astype(jnp.float32))
    return q, k, v, logf


def fox_prompt(h, w_in, b_f, w_out):
    bs, t, _ = h.shape
    q, k, v, logf = fox_project(h, w_in, b_f)
    n_blk = -(-t // Q_BLOCK)
    tp = n_blk * Q_BLOCK
    pad4 = ((0, 0), (0, tp - t), (0, 0), (0, 0))
    qp, kp, vp = jnp.pad(q, pad4), jnp.pad(k, pad4), jnp.pad(v, pad4)
    cum = jnp.swapaxes(jnp.cumsum(jnp.pad(logf, ((0, 0), (0, tp - t), (0, 0))), axis=1), 1, 2)
    key_pos = jnp.arange(tp)

    def block(start):
        qb = lax.dynamic_slice_in_dim(qp, start, Q_BLOCK, axis=1)
        cb = lax.dynamic_slice_in_dim(cum, start, Q_BLOCK, axis=2)
        sc = jnp.einsum('bqhd,bkhd->bhqk', qb, kp).astype(jnp.float32) * FOX_SCALE \
            + (cb[..., :, None] - cum[..., None, :])
        mask = (start + jnp.arange(Q_BLOCK))[:, None] >= key_pos[None, :]
        p = jax.nn.softmax(jnp.where(mask, sc, -jnp.inf), axis=-1)
        return jnp.einsum('bhqk,bkhd->bqhd', p.astype(vp.dtype), vp)

    o = lax.map(block, jnp.arange(n_blk) * Q_BLOCK)
    o = jnp.moveaxis(o, 0, 1).reshape(bs, tp, FOX_HD)[:, :t]
    return o @ w_out, k, v, logf


def fox_sample(h, k_pool, v_pool, lf_pool, page_table, w_in, b_f, w_out):
    bs, sn, _ = h.shape
    q, k, v, logf = fox_project(h, w_in, b_f)
    k_past = k_pool[page_table].reshape(bs, -1, FOX_HEADS, FOX_HEAD_DIM).astype(k.dtype)
    v_past = v_pool[page_table].reshape(bs, -1, FOX_HEADS, FOX_HEAD_DIM).astype(v.dtype)
    lf_past = lf_pool[page_table].reshape(bs, -1, FOX_HEADS).astype(jnp.float32)
    n_past = k_past.shape[1]
    c_past = jnp.cumsum(lf_past, axis=1)
    c_new = c_past[:, -1:] + jnp.cumsum(logf, axis=1)
    cq = jnp.swapaxes(c_new, 1, 2)
    cp = jnp.swapaxes(c_past, 1, 2)
    s_past = jnp.einsum('bqhd,bkhd->bhqk', q, k_past).astype(jnp.float32) * FOX_SCALE \
        + (cq[..., :, None] - cp[..., None, :])
    s_new = jnp.einsum('bqhd,bkhd->bhqk', q, k).astype(jnp.float32) * FOX_SCALE \
        + (cq[..., :, None] - cq[..., None, :])
    s_new = jnp.where(jnp.tril(jnp.ones((sn, sn), dtype=bool)), s_new, -jnp.inf)
    p = jax.nn.softmax(jnp.concatenate([s_past, s_new], axis=-1), axis=-1)
    o = jnp.einsum('bhqk,bkhd->bqhd', p[..., :n_past].astype(v.dtype), v_past) \
        + jnp.einsum('bhqk,bkhd->bqhd', p[..., n_past:].astype(v.dtype), v)
    return o.reshape(bs, sn, FOX_HD) @ w_out, k, v, logf


def peer_ffn(h, w_q, sub_keys, u, v):
    bs, t, d = h.shape
    n = bs * t
    n_pad = -(-n // PEER_BLOCK) * PEER_BLOCK
    xb = jnp.pad(h.reshape(n, d), ((0, n_pad - n), (0, 0))).reshape(n_pad // PEER_BLOCK, PEER_BLOCK, d)
    keys = sub_keys.astype(jnp.float32)

    def block(xt):
        q = (xt @ w_q).astype(jnp.float32).reshape(PEER_BLOCK, PEER_HEADS, 2, PEER_HALF)
        sc = jnp.einsum('nhpd,pkd->nhpk', q, keys)
        ts, ti = lax.top_k(sc, PEER_TOPK)
        cand_s = (ts[:, :, 0, :, None] + ts[:, :, 1, None, :]).reshape(PEER_BLOCK, PEER_HEADS, PEER_TOPK * PEER_TOPK)
        cand_i = (ti[:, :, 0, :, None] * N_KEYS + ti[:, :, 1, None, :]).reshape(PEER_BLOCK, PEER_HEADS, PEER_TOPK * PEER_TOPK)
        best_s, best_pos = lax.top_k(cand_s, PEER_TOPK)
        idx = jnp.take_along_axis(cand_i, best_pos, axis=-1)
        gate = jax.nn.softmax(best_s, axis=-1)
        act = jax.nn.gelu(jnp.einsum('nd,nhkd->nhk', xt, u[idx]).astype(jnp.float32), approximate=False)
        return jnp.einsum('nhk,nhkd->nd', (gate * act).astype(xt.dtype), v[idx])

    y = lax.map(block, xb)
    return y.reshape(n_pad, d)[:n].reshape(bs, t, d)


def setup_inputs(seed: int = 0) -> dict:
    key = jax.random.key(seed)
    ks = jax.random.split(key, 32)
    f32 = jnp.float32
    nrm = lambda kk, shape, scale: scale * jax.random.normal(kk, shape, f32)
    n_pages = PAST_LEN // PAGE_SIZE
    n_used = DEC_BATCH * n_pages
    n_pool = n_used + (n_used + 3) // 4
    page_table = jax.random.permutation(ks[0], n_pool)[:n_used].reshape(DEC_BATCH, n_pages).astype(jnp.int32)
    dt = jnp.exp(jax.random.uniform(ks[1], (N_GDN_LAYERS, GDN_HEADS), f32, math.log(1e-3), math.log(1e-1)))
    dt_bias = dt + jnp.log(-jnp.expm1(-dt))
    a_log = jnp.log(jax.random.uniform(ks[2], (N_GDN_LAYERS, GDN_HEADS), f32, 1.0, 16.0))
    return {
        'x_prompt': nrm(ks[3], (BATCH, SEQ, D_MODEL), 1.0),
        'x_sample': nrm(ks[4], (DEC_BATCH, DEC_SEQ, D_MODEL), 1.0),
        'state_conv': nrm(ks[5], (N_GDN_LAYERS, DEC_BATCH, CONV_W - 1, 3 * GDN_HD), 1.0),
        'state_ssm': nrm(ks[6], (N_GDN_LAYERS, DEC_BATCH, GDN_HEADS, GDN_HEAD_DIM, GDN_HEAD_DIM), 0.1),
        'cache_k': nrm(ks[7], (N_FOX_LAYERS, n_pool, PAGE_SIZE, FOX_HEADS, FOX_HEAD_DIM), 1.0),
        'cache_v': nrm(ks[8], (N_FOX_LAYERS, n_pool, PAGE_SIZE, FOX_HEADS, FOX_HEAD_DIM), 1.0),
        'cache_logf': jax.nn.log_sigmoid(2.0 + nrm(ks[9], (N_FOX_LAYERS, n_pool, PAGE_SIZE, FOX_HEADS), 1.0)),
        'page_table': page_table,
        'meta_tokens': nrm(ks[10], (N_META, D_MODEL), 1.0),
        'norm_mix': 1.0 + nrm(ks[11], (DEPTH, D_MODEL), 0.1),
        'norm_ffn': 1.0 + nrm(ks[12], (DEPTH, D_MODEL), 0.1),
        'norm_final': 1.0 + nrm(ks[13], (D_MODEL,), 0.1),
        'gdn_w_in': nrm(ks[14], (N_GDN_LAYERS, D_MODEL, GDN_IN), D_MODEL ** -0.5),
        'gdn_conv_w': nrm(ks[15], (N_GDN_LAYERS, CONV_W, 3 * GDN_HD), CONV_W ** -0.5),
        'gdn_a_log': a_log,
        'gdn_dt_bias': dt_bias,
        'gdn_out_norm': 1.0 + nrm(ks[16], (N_GDN_LAYERS, GDN_HEAD_DIM), 0.1),
        'gdn_w_out': nrm(ks[17], (N_GDN_LAYERS, GDN_HD, D_MODEL), GDN_HD ** -0.5),
        'fox_w_in': nrm(ks[18], (N_FOX_LAYERS, D_MODEL, FOX_IN), D_MODEL ** -0.5),
        'fox_b_f': 2.0 + nrm(ks[19], (N_FOX_LAYERS, FOX_HEADS), 0.5),
        'fox_w_out': nrm(ks[20], (N_FOX_LAYERS, FOX_HD, D_MODEL), FOX_HD ** -0.5),
        'peer_w_q': nrm(ks[21], (DEPTH, D_MODEL, PEER_HEADS * PEER_QUERY_DIM), D_MODEL ** -0.5),
        'peer_sub_keys': nrm(ks[22], (DEPTH, 2, N_KEYS, PEER_HALF), PEER_HALF ** -0.5),
        'peer_u': nrm(ks[23], (DEPTH, N_EXPERTS, D_MODEL), D_MODEL ** -0.5),
        'peer_v': nrm(ks[24], (DEPTH, N_EXPERTS, D_MODEL), PEER_HEADS ** -0.5),
    }


def reference(x_prompt, x_sample, state_conv, state_ssm, cache_k, cache_v, cache_logf, page_table,
              meta_tokens, norm_mix, norm_ffn, norm_final, gdn_w_in, gdn_conv_w, gdn_a_log, gdn_dt_bias,
              gdn_out_norm, gdn_w_out, fox_w_in, fox_b_f, fox_w_out, peer_w_q, peer_sub_keys, peer_u, peer_v):
    bp = x_prompt.shape[0]
    meta = jnp.broadcast_to(meta_tokens.astype(x_prompt.dtype)[None], (bp, N_META, D_MODEL))
    h_p = jnp.concatenate([meta, x_prompt], axis=1)
    h_s = x_sample
    conv_p, ssm_p, k_p, v_p, lf_p = [], [], [], [], []
    conv_s, ssm_s, k_s, v_s, lf_s = [], [], [], [], []
    for i in range(DEPTH):
        j = i // N_MIXERS
        hn_p = rms_norm(h_p, norm_mix[i])
        hn_s = rms_norm(h_s, norm_mix[i])
        if i % N_MIXERS == 0:
            m_p, c_new, s_new = gdn_prompt(hn_p, gdn_w_in[j], gdn_conv_w[j], gdn_a_log[j], gdn_dt_bias[j],
                                           gdn_out_norm[j], gdn_w_out[j])
            conv_p.append(c_new)
            ssm_p.append(s_new)
            m_s, c_new, s_new = gdn_sample(hn_s, state_conv[j], state_ssm[j], gdn_w_in[j], gdn_conv_w[j],
                                           gdn_a_log[j], gdn_dt_bias[j], gdn_out_norm[j], gdn_w_out[j])
            conv_s.append(c_new)
            ssm_s.append(s_new)
        else:
            m_p, kk, vv, lf = fox_prompt(hn_p, fox_w_in[j], fox_b_f[j], fox_w_out[j])
            k_p.append(kk)
            v_p.append(vv)
            lf_p.append(lf)
            m_s, kk, vv, lf = fox_sample(hn_s, cache_k[j], cache_v[j], cache_logf[j], page_table,
                                         fox_w_in[j], fox_b_f[j], fox_w_out[j])
            k_s.append(kk)
            v_s.append(vv)
            lf_s.append(lf)
        h_p = h_p + m_p
        h_s = h_s + m_s
        h_p = h_p + peer_ffn(rms_norm(h_p, norm_ffn[i]), peer_w_q[i], peer_sub_keys[i], peer_u[i], peer_v[i])
        h_s = h_s + peer_ffn(rms_norm(h_s, norm_ffn[i]), peer_w_q[i], peer_sub_keys[i], peer_u[i], peer_v[i])
    y_prompt = rms_norm(h_p, norm_final)[:, N_META:]
    y_sample = rms_norm(h_s, norm_final)
    new_conv_p = jnp.stack(conv_p, 0)
    new_ssm_p = jnp.stack(ssm_p, 0)
    new_k_p = jnp.stack(k_p, 0)
    new_v_p = jnp.stack(v_p, 0)
    new_logf_p = jnp.stack(lf_p, 0)
    new_conv_s = jnp.stack(conv_s, 0)
    new_ssm_s = jnp.stack(ssm_s, 0)
    new_k_s = jnp.stack(k_s, 0)
    new_v_s = jnp.stack(v_s, 0)
    new_logf_s = jnp.stack(lf_s, 0)
    return (y_prompt, y_sample, new_conv_p, new_ssm_p, new_k_p, new_v_p, new_logf_p,
            new_conv_s, new_ssm_s, new_k_s, new_v_s, new_logf_s)
```

```python
import functools
import math

import jax
import jax.numpy as jnp
from jax import lax
from jax.experimental import pallas as pl
from jax.experimental.pallas import tpu as pltpu

F32 = jnp.float32
BF16 = jnp.bfloat16
I32 = jnp.int32
HI = lax.Precision.HIGHEST

EPS = 1e-6
LANES = 128
SUBLANES = 8
VMEM_LIMIT = 56 * 1024 * 1024

N_META = 16
LEAD_PAD = 112
HEAD_DIM = 128
N_HEADS = 8
CONV_W = 4
GDN_CHUNK = 64
INV_BLOCK = 16
PAGE = 128
PEER_TOPK = 16
N_KEYS = 128
PEER_SLOTS = N_HEADS * PEER_TOPK
ROW_WORDS = 4
NEG = -1e30


def _cparams(sem):
    return pltpu.CompilerParams(dimension_semantics=sem, vmem_limit_bytes=VMEM_LIMIT)


def _dot(a, b, precision=None):
    return jnp.dot(a, b, preferred_element_type=F32, precision=precision)


def _dot_nt(a, b, precision=None):
    return lax.dot_general(a, b, (((1,), (1,)), ((), ())), preferred_element_type=F32, precision=precision)


def _dot_tn(a, b, precision=None):
    return lax.dot_general(a, b, (((0,), (0,)), ((), ())), preferred_element_type=F32, precision=precision)


def _split_dot(a, b01):
    hi = a.astype(BF16)
    lo = (a - hi.astype(F32)).astype(BF16)
    return _dot(hi, b01) + _dot(lo, b01)


def _softplus(x):
    return jnp.maximum(x, 0.0) + jnp.log1p(jnp.exp(-jnp.abs(x)))


def _log_sigmoid(x):
    return -_softplus(-x)


def _gelu_exact(x):
    return 0.5 * x * (1.0 + lax.erf(x * (2.0 ** -0.5)))


def _norm_matmul_kernel(x_ref, g_ref, w_ref, o_ref, *xn_ref):
    x = x_ref[...]
    xn = x * lax.rsqrt(jnp.mean(x * x, axis=-1, keepdims=True) + EPS) * g_ref[...]
    if xn_ref:
        xn_ref[0][...] = xn
    o_ref[...] = _dot(xn.astype(BF16), w_ref[...])


def norm_matmul(x, g, w_bf16, tm=256, with_xn=False):
    n, k = x.shape
    m = w_bf16.shape[1]
    out_shape = [jax.ShapeDtypeStruct((n, m), F32)]
    out_specs = [pl.BlockSpec((tm, m), lambda i: (i, 0))]
    if with_xn:
        out_shape.append(jax.ShapeDtypeStruct((n, k), F32))
        out_specs.append(pl.BlockSpec((tm, k), lambda i: (i, 0)))
    res = pl.pallas_call(
        _norm_matmul_kernel,
        grid=(n // tm,),
        in_specs=[pl.BlockSpec((tm, k), lambda i: (i, 0)),
                  pl.BlockSpec((1, k), lambda i: (0, 0)),
                  pl.BlockSpec((k, m), lambda i: (0, 0))],
        out_specs=out_specs,
        out_shape=out_shape,
        compiler_params=_cparams(("parallel",)),
        name="norm_matmul",
    )(x, g.reshape(1, k), w_bf16)
    return res if with_xn else res[0]


def _matmul_res_kernel(x_ref, w_ref, r_ref, o_ref):
    o_ref[...] = r_ref[...] + _dot(x_ref[...].astype(BF16), w_ref[...])


def matmul_residual(x, w_bf16, res, tm=256):
    n, k = x.shape
    m = w_bf16.shape[1]
    return pl.pallas_call(
        _matmul_res_kernel,
        grid=(n // tm,),
        in_specs=[pl.BlockSpec((tm, k), lambda i: (i, 0)),
                  pl.BlockSpec((k, m), lambda i: (0, 0)),
                  pl.BlockSpec((tm, m), lambda i: (i, 0))],
        out_specs=pl.BlockSpec((tm, m), lambda i: (i, 0)),
        out_shape=jax.ShapeDtypeStruct((n, m), F32),
        compiler_params=_cparams(("parallel",)),
        name="matmul_residual",
    )(x, w_bf16, res)


def _gdn_out_kernel(o_ref, gate_ref, nw_ref, w_ref, r_ref, y_ref):
    o = o_ref[...]
    gate = gate_ref[...]
    nw = nw_ref[...]
    parts = []
    for h in range(N_HEADS):
        sl = slice(h * HEAD_DIM, (h + 1) * HEAD_DIM)
        oh = o[:, sl]
        oh = oh * lax.rsqrt(jnp.mean(oh * oh, axis=-1, keepdims=True) + EPS) * nw
        parts.append(oh * jax.nn.silu(gate[:, sl]))
    x = jnp.concatenate(parts, axis=-1)
    y_ref[...] = r_ref[...] + _dot(x.astype(BF16), w_ref[...])


def gdn_out(o, z, gate_col_block, out_norm, w_bf16, res, tm=256):
    n, d = o.shape
    return pl.pallas_call(
        _gdn_out_kernel,
        grid=(n // tm,),
        in_specs=[pl.BlockSpec((tm, d), lambda i: (i, 0)),
                  pl.BlockSpec((tm, d), lambda i: (i, gate_col_block)),
                  pl.BlockSpec((1, HEAD_DIM), lambda i: (0, 0)),
                  pl.BlockSpec((d, d), lambda i: (0, 0)),
                  pl.BlockSpec((tm, d), lambda i: (i, 0))],
        out_specs=pl.BlockSpec((tm, d), lambda i: (i, 0)),
        out_shape=jax.ShapeDtypeStruct((n, d), F32),
        compiler_params=_cparams(("parallel",)),
        name="gdn_out",
    )(o, z, out_norm.reshape(1, HEAD_DIM), w_bf16, res)


def _final_norm_kernel(x_ref, g_ref, o_ref):
    x = x_ref[...]
    o_ref[...] = x * lax.rsqrt(jnp.mean(x * x, axis=-1, keepdims=True) + EPS) * g_ref[...]


def final_norm(x3, g, row_block_offset, t_out, tb=128):
    b, _, d = x3.shape
    return pl.pallas_call(
        _final_norm_kernel,
        grid=(b, t_out // tb),
        in_specs=[pl.BlockSpec((None, tb, d), lambda i, j: (i, j + row_block_offset, 0)),
                  pl.BlockSpec((1, d), lambda i, j: (0, 0))],
        out_specs=pl.BlockSpec((None, tb, d), lambda i, j: (i, j, 0)),
        out_shape=jax.ShapeDtypeStruct((b, t_out, d), F32),
        compiler_params=_cparams(("parallel", "parallel")),
        name="final_norm",
    )(x3, g.reshape(1, d))


def _gdn_prep_kernel(prev_ref, cur_ref, ab_ref, convw_ref, par_ref, qkv_ref, gb_ref, *,
                     tb, zero_first, t_lo, t_hi):
    t = pl.program_id(1)
    cur = cur_ref[...]
    prev = prev_ref[...]
    if zero_first:
        prev = jnp.where(t == 0, 0.0, prev)
    cat = jnp.concatenate([prev, cur], axis=0)
    w = convw_ref[...]
    y = None
    for i in range(CONV_W):
        shift = CONV_W - 1 - i
        xi = cur if shift == 0 else pltpu.roll(cat, shift, axis=0)[SUBLANES:]
        term = xi * w[i:i + 1]
        y = term if y is None else y + term
    y = jax.nn.silu(y)
    hd = N_HEADS * HEAD_DIM
    for h in range(2 * N_HEADS):
        sl = slice(h * HEAD_DIM, (h + 1) * HEAD_DIM)
        xh = y[:, sl]
        xh = xh * lax.rsqrt(jnp.sum(xh * xh, axis=-1, keepdims=True) + EPS)
        if h < N_HEADS:
            xh = xh * (HEAD_DIM ** -0.5)
        qkv_ref[:, sl] = xh
    qkv_ref[:, 2 * hd:] = y[:, 2 * hd:]
    ab = ab_ref[...]
    a_log = par_ref[0:1, :]
    dt_bias = par_ref[1:2, :]
    g = -jnp.exp(a_log) * _softplus(ab + dt_bias)
    beta = jax.nn.sigmoid(ab)
    lane = lax.broadcasted_iota(I32, ab.shape, 1)
    pos = t * tb + lax.broadcasted_iota(I32, ab.shape, 0)
    valid = (pos >= t_lo) & (pos < t_hi)
    gb = jnp.where(lane < N_HEADS, g, jnp.where(lane < 2 * N_HEADS, beta, 0.0))
    gb_ref[...] = jnp.where(valid, gb, 0.0)


def gdn_prep(z3, prev_src, conv_w, a_log, dt_bias, tb, zero_first, t_lo, t_hi):
    b, t, _ = z3.shape
    c = 3 * N_HEADS * HEAD_DIM
    ab_block = (4 * N_HEADS * HEAD_DIM) // LANES
    par = jnp.zeros((SUBLANES, LANES), F32)
    par = par.at[0, :N_HEADS].set(a_log).at[1, :N_HEADS].set(dt_bias)
    if zero_first:
        prev_map = lambda i, j: (i, jnp.maximum(j * (tb // SUBLANES) - 1, 0), 0)
    else:
        prev_map = lambda i, j: (i, 0, 0)
    kern = functools.partial(_gdn_prep_kernel, tb=tb, zero_first=zero_first, t_lo=t_lo, t_hi=t_hi)
    return pl.pallas_call(
        kern,
        grid=(b, t // tb),
        in_specs=[pl.BlockSpec((None, SUBLANES, c), prev_map),
                  pl.BlockSpec((None, tb, c), lambda i, j: (i, j, 0)),
                  pl.BlockSpec((None, tb, LANES), lambda i, j: (i, j, ab_block)),
                  pl.BlockSpec((CONV_W, c), lambda i, j: (0, 0)),
                  pl.BlockSpec((SUBLANES, LANES), lambda i, j: (0, 0))],
        out_specs=[pl.BlockSpec((None, tb, c), lambda i, j: (i, j, 0)),
                   pl.BlockSpec((None, tb, LANES), lambda i, j: (i, j, 0))],
        out_shape=[jax.ShapeDtypeStruct((b, t, c), F32),
                   jax.ShapeDtypeStruct((b, t, LANES), F32)],
        compiler_params=_cparams(("parallel", "arbitrary")),
        name="gdn_prep",
    )(prev_src, z3, z3, conv_w, par)


def _unit_lower_inverse(a, n):
    ii = lax.broadcasted_iota(I32, (n, n), 0)
    jj = lax.broadcasted_iota(I32, (n, n), 1)
    eye = (ii == jj).astype(F32)
    blk = min(INV_BLOCK, n)
    if n > blk:
        same = (ii // blk) == (jj // blk)
        d = jnp.where(same, a, 0.0)
        rest = jnp.where(same, 0.0, a)
    else:
        d = a
    x = eye - d
    p = d
    width = 1
    while 2 * width < blk:
        p = _dot(p, p, HI)
        x = x + _dot(x, p, HI)
        width *= 2
    if n == blk:
        return x
    nm = _dot(x, rest, HI)
    y = eye - nm
    p = nm
    width = 1
    while 2 * width < n // blk:
        p = _dot(p, p, HI)
        y = y + _dot(y, p, HI)
        width *= 2
    return _dot(y, x, HI)


def _gdn_chunk_kernel(qkv_ref, gb_ref, gbt_ref, s0_ref, o_ref, sfin_ref, state_ref, *, ln):
    c = pl.program_id(1)

    @pl.when(c == 0)
    def _():
        state_ref[...] = s0_ref[...]

    hd = N_HEADS * HEAD_DIM
    ii = lax.broadcasted_iota(I32, (ln, ln), 0)
    jj = lax.broadcasted_iota(I32, (ln, ln), 1)
    tri_l = (ii >= jj).astype(F32)
    gb = gb_ref[...]
    gbt = gbt_ref[...]
    gc_cols = _dot(tri_l, gb, HI)
    gc_rows = _dot_nt(gbt, tri_l, HI)
    causal = ii >= jj
    strict = ii > jj
    for h in range(N_HEADS):
        q = qkv_ref[:, h * HEAD_DIM:(h + 1) * HEAD_DIM]
        k = qkv_ref[:, hd + h * HEAD_DIM:hd + (h + 1) * HEAD_DIM]
        v = qkv_ref[:, 2 * hd + h * HEAD_DIM:2 * hd + (h + 1) * HEAD_DIM]
        beta = gb[:, N_HEADS + h:N_HEADS + h + 1]
        gc = gc_cols[:, h:h + 1]
        gr = gc_rows[h:h + 1, :]
        decay = jnp.where(causal, jnp.exp(jnp.where(causal, gc - gr, 0.0)), 0.0)
        kb = k * beta
        a = jnp.where(strict, _dot_nt(kb, k, HI) * decay, 0.0)
        tinv = _unit_lower_inverse(a, ln)
        egc = jnp.exp(gc)
        u = _dot(tinv, v * beta, HI)
        w = _dot(tinv, kb * egc, HI)
        s = state_ref[h]
        v_new = u - _dot(w, s, HI)
        attn = _dot_nt(q, k, HI) * decay
        o = _dot(q * egc, s, HI) + _dot(attn, v_new, HI)
        o_ref[:, h * HEAD_DIM:(h + 1) * HEAD_DIM] = o
        g_last = gc[ln - 1:ln, :]
        state_ref[h] = s * jnp.exp(g_last) + _dot_tn(k * jnp.exp(g_last - gc), v_new, HI)

    @pl.when(c == pl.num_programs(1) - 1)
    def _():
        sfin_ref[...] = state_ref[...]


def gdn_chunks(qkv_c, gb, s0, ln):
    b, t, c = qkv_c.shape
    nc = t // ln
    gbt = jnp.swapaxes(gb[..., :2 * N_HEADS].reshape(b, nc, ln, 2 * N_HEADS), 2, 3)
    hd = N_HEADS * HEAD_DIM
    kern = functools.partial(_gdn_chunk_kernel, ln=ln)
    return pl.pallas_call(
        kern,
        grid=(b, nc),
        in_specs=[pl.BlockSpec((None, ln, c), lambda i, j: (i, j, 0)),
                  pl.BlockSpec((None, ln, LANES), lambda i, j: (i, j, 0)),
                  pl.BlockSpec((None, None, 2 * N_HEADS, ln), lambda i, j: (i, j, 0, 0)),
                  pl.BlockSpec((None, N_HEADS, HEAD_DIM, HEAD_DIM), lambda i, j: (i, 0, 0, 0))],
        out_specs=[pl.BlockSpec((None, ln, hd), lambda i, j: (i, j, 0)),
                   pl.BlockSpec((None, N_HEADS, HEAD_DIM, HEAD_DIM), lambda i, j: (i, 0, 0, 0))],
        out_shape=[jax.ShapeDtypeStruct((b, t, hd), F32),
                   jax.ShapeDtypeStruct((b, N_HEADS, HEAD_DIM, HEAD_DIM), F32)],
        scratch_shapes=[pltpu.VMEM((N_HEADS, HEAD_DIM, HEAD_DIM), F32)],
        compiler_params=_cparams(("parallel", "arbitrary")),
        name="gdn_chunks",
    )(qkv_c, gb, gbt, s0)


def _logf_cumsum_kernel(z_ref, bf_ref, lf_ref, cum_ref, carry_ref, *, tb):
    @pl.when(pl.program_id(1) == 0)
    def _():
        carry_ref[...] = jnp.zeros_like(carry_ref)

    lf = _log_sigmoid(z_ref[...] + bf_ref[...])
    lf_ref[...] = lf
    ii = lax.broadcasted_iota(I32, (tb, tb), 0)
    jj = lax.broadcasted_iota(I32, (tb, tb), 1)
    cum = _dot((ii >= jj).astype(F32), lf, HI) + carry_ref[...]
    cum_ref[...] = cum
    carry_ref[...] = cum[tb - 1:tb, :]


def logf_cumsum(z3, b_f, col_block, tb):
    b, t, _ = z3.shape
    bf = jnp.zeros((1, LANES), F32).at[0, :N_HEADS].set(b_f)
    kern = functools.partial(_logf_cumsum_kernel, tb=tb)
    return pl.pallas_call(
        kern,
        grid=(b, t // tb),
        in_specs=[pl.BlockSpec((None, tb, LANES), lambda i, j: (i, j, col_block)),
                  pl.BlockSpec((1, LANES), lambda i, j: (0, 0))],
        out_specs=[pl.BlockSpec((None, tb, LANES), lambda i, j: (i, j, 0)),
                   pl.BlockSpec((None, tb, LANES), lambda i, j: (i, j, 0))],
        out_shape=[jax.ShapeDtypeStruct((b, t, LANES), F32),
                   jax.ShapeDtypeStruct((b, t, LANES), F32)],
        scratch_shapes=[pltpu.VMEM((1, LANES), F32)],
        compiler_params=_cparams(("parallel", "arbitrary")),
        name="logf_cumsum",
    )(z3, bf)


def _fox_prompt_kernel(q_ref, k_ref, v_ref, cq_ref, ck_ref, o_ref, m_ref, l_ref, acc_ref, *, tq, first_key):
    qi = pl.program_id(1)
    ki = pl.program_id(2)

    @pl.when(ki == 0)
    def _():
        m_ref[...] = jnp.full_like(m_ref, NEG)
        l_ref[...] = jnp.zeros_like(l_ref)
        acc_ref[...] = jnp.zeros_like(acc_ref)

    @pl.when(ki <= qi)
    def _():
        qpos = qi * tq + lax.broadcasted_iota(I32, (tq, tq), 0)
        kpos = ki * tq + lax.broadcasted_iota(I32, (tq, tq), 1)
        mask = (qpos >= kpos) & (kpos >= first_key)
        cq = cq_ref[...]
        ck = ck_ref[...]
        for h in range(N_HEADS):
            sl = slice(h * HEAD_DIM, (h + 1) * HEAD_DIM)
            s = _dot_nt(q_ref[:, sl].astype(BF16), k_ref[:, sl].astype(BF16)) * (HEAD_DIM ** -0.5)
            s = s + (cq[:, h:h + 1] - ck[h:h + 1, :])
            s = jnp.where(mask, s, NEG)
            m_old = m_ref[h]
            m_new = jnp.maximum(m_old, jnp.max(s, axis=-1, keepdims=True))
            alpha = jnp.exp(m_old - m_new)
            p = jnp.exp(s - m_new)
            l_ref[h] = alpha * l_ref[h] + jnp.sum(p, axis=-1, keepdims=True)
            acc_ref[:, sl] = alpha * acc_ref[:, sl] + _dot(p.astype(BF16), v_ref[:, sl].astype(BF16))
            m_ref[h] = m_new

    @pl.when(ki == qi)
    def _():
        for h in range(N_HEADS):
            sl = slice(h * HEAD_DIM, (h + 1) * HEAD_DIM)
            o_ref[:, sl] = acc_ref[:, sl] / l_ref[h]


def fox_prompt(z3, cum, tq, first_key):
    b, t, _ = z3.shape
    hd = N_HEADS * HEAD_DIM
    cum_t = jnp.swapaxes(cum[..., :N_HEADS], 1, 2)
    nq = t // tq
    kern = functools.partial(_fox_prompt_kernel, tq=tq, first_key=first_key)
    kmap = lambda col: (lambda i, qi, ki: (i, jnp.minimum(ki, qi), col))
    return pl.pallas_call(
        kern,
        grid=(b, nq, nq),
        in_specs=[pl.BlockSpec((None, tq, hd), lambda i, qi, ki: (i, qi, 0)),
                  pl.BlockSpec((None, tq, hd), kmap(1)),
                  pl.BlockSpec((None, tq, hd), kmap(2)),
                  pl.BlockSpec((None, tq, LANES), lambda i, qi, ki: (i, qi, 0)),
                  pl.BlockSpec((None, N_HEADS, tq), lambda i, qi, ki: (i, 0, jnp.minimum(ki, qi)))],
        out_specs=pl.BlockSpec((None, tq, hd), lambda i, qi, ki: (i, qi, 0)),
        out_shape=jax.ShapeDtypeStruct((b, t, hd), F32),
        scratch_shapes=[pltpu.VMEM((N_HEADS, tq, 1), F32),
                        pltpu.VMEM((N_HEADS, tq, 1), F32),
                        pltpu.VMEM((tq, hd), F32)],
        compiler_params=_cparams(("parallel", "parallel", "arbitrary")),
        name="fox_prompt",
    )(z3, z3, z3, cum, cum_t)


def _lane_cumsum(x):
    lane = lax.broadcasted_iota(I32, x.shape, 1)
    sh = 1
    while sh < LANES:
        x = x + jnp.where(lane >= sh, pltpu.roll(x, sh, axis=1), 0.0)
        sh *= 2
    return x


def _fox_sample_kernel(pt_ref, q_ref, kn_ref, vn_ref, lfn_ref, kp_ref, vp_ref, lfp_ref, o_ref,
                       qbd_ref, m_ref, l_ref, acc_ref, carry_ref, *, sn):
    del pt_ref
    p = pl.program_id(1)
    rows = sn * N_HEADS
    hd = N_HEADS * HEAD_DIM
    head_of_col = lax.broadcasted_iota(I32, (N_HEADS, hd), 1) // HEAD_DIM
    head_mask = (head_of_col == lax.broadcasted_iota(I32, (N_HEADS, hd), 0)).astype(F32)

    @pl.when(p == 0)
    def _():
        m_ref[...] = jnp.full_like(m_ref, NEG)
        l_ref[...] = jnp.zeros_like(l_ref)
        acc_ref[...] = jnp.zeros_like(acc_ref)
        carry_ref[...] = jnp.zeros_like(carry_ref)
        q = q_ref[...]
        for s in range(sn):
            qbd_ref[s * N_HEADS:(s + 1) * N_HEADS, :] = (
                jnp.broadcast_to(q[s:s + 1, :], (N_HEADS, hd)) * head_mask)

    def update(scores, vals):
        m_old = m_ref[...]
        m_new = jnp.maximum(m_old, jnp.max(scores, axis=-1, keepdims=True))
        alpha = jnp.exp(m_old - m_new)
        pr = jnp.exp(scores - m_new)
        l_ref[...] = alpha * l_ref[...] + jnp.sum(pr, axis=-1, keepdims=True)
        acc_ref[...] = alpha * acc_ref[...] + _dot(pr.astype(BF16), vals)
        m_ref[...] = m_new

    qbd = qbd_ref[...].astype(BF16)
    cum = _lane_cumsum(lfp_ref[...]) + carry_ref[...]
    sc = _dot_nt(qbd, kp_ref[...].astype(BF16)) * (HEAD_DIM ** -0.5)
    sc = sc - jnp.concatenate([cum] * sn, axis=0)
    update(sc, vp_ref[...].astype(BF16))
    carry_ref[...] = cum[:, LANES - 1:LANES]

    @pl.when(p == pl.num_programs(1) - 1)
    def _():
        pad = jnp.zeros((PAGE - sn, hd), F32)
        kn = jnp.concatenate([kn_ref[...], pad], axis=0).astype(BF16)
        vn = jnp.concatenate([vn_ref[...], pad], axis=0).astype(BF16)
        cnew = _lane_cumsum(lfn_ref[...]) + carry_ref[...]
        scn = _dot_nt(qbd, kn) * (HEAD_DIM ** -0.5) - jnp.concatenate([cnew] * sn, axis=0)
        qs = lax.broadcasted_iota(I32, (rows, PAGE), 0) // N_HEADS
        kj = lax.broadcasted_iota(I32, (rows, PAGE), 1)
        scn = jnp.where(kj <= qs, scn, NEG)
        update(scn, vn)
        out = acc_ref[...] / l_ref[...]
        out = out * jnp.concatenate([head_mask] * sn, axis=0)
        o_ref[...] = jnp.sum(out.reshape(sn, N_HEADS, hd), axis=1)


def fox_sample(z_s, lfn_t, k_pool, v_pool, lf_pool_t, page_table):
    b, sn, _ = z_s.shape
    n_pages = page_table.shape[1]
    hd = N_HEADS * HEAD_DIM
    rows = sn * N_HEADS
    kern = functools.partial(_fox_sample_kernel, sn=sn)
    grid_spec = pltpu.PrefetchScalarGridSpec(
        num_scalar_prefetch=1,
        grid=(b, n_pages),
        in_specs=[pl.BlockSpec((None, sn, hd), lambda i, p, pt: (i, 0, 0)),
                  pl.BlockSpec((None, sn, hd), lambda i, p, pt: (i, 0, 1)),
                  pl.BlockSpec((None, sn, hd), lambda i, p, pt: (i, 0, 2)),
                  pl.BlockSpec((None, N_HEADS, LANES), lambda i, p, pt: (i, 0, 0)),
                  pl.BlockSpec((None, PAGE, hd), lambda i, p, pt: (pt[i, p], 0, 0)),
                  pl.BlockSpec((None, PAGE, hd), lambda i, p, pt: (pt[i, p], 0, 0)),
                  pl.BlockSpec((None, N_HEADS, PAGE), lambda i, p, pt: (pt[i, p], 0, 0))],
        out_specs=pl.BlockSpec((None, sn, hd), lambda i, p, pt: (i, 0, 0)),
        scratch_shapes=[pltpu.VMEM((rows, hd), F32),
                        pltpu.VMEM((rows, 1), F32),
                        pltpu.VMEM((rows, 1), F32),
                        pltpu.VMEM((rows, hd), F32),
                        pltpu.VMEM((N_HEADS, 1), F32)],
    )
    return pl.pallas_call(
        kern,
        grid_spec=grid_spec,
        out_shape=jax.ShapeDtypeStruct((b, sn, hd), F32),
        compiler_params=_cparams(("parallel", "arbitrary")),
        name="fox_sample",
    )(page_table, z_s, z_s, z_s, lfn_t, k_pool, v_pool, lf_pool_t)


def _top_rows(x, payload, k):
    r = x.shape[0]
    row = lax.broadcasted_iota(I32, x.shape, 0)
    vals, picks = [], []
    for _ in range(k):
        m = jnp.max(x, axis=0, keepdims=True)
        pos = jnp.min(jnp.where(x == m, row, r), axis=0, keepdims=True)
        sel = row == pos
        vals.append(m)
        picks.append(pos if payload is None else jnp.max(jnp.where(sel, payload, -1), axis=0, keepdims=True))
        x = jnp.where(sel, -jnp.inf, x)
    return jnp.concatenate(vals, axis=0), jnp.concatenate(picks, axis=0)


def _peer_topk_kernel(q_ref, keys_ref, idx_ref, gate_ref):
    for h in range(N_HEADS):
        ts, ti = [], []
        for half in range(2):
            col = (2 * h + half) * N_KEYS
            sc = _dot_nt(keys_ref[half], q_ref[:, col:col + N_KEYS], HI)
            v, i = _top_rows(sc, None, PEER_TOPK)
            ts.append(v)
            ti.append(i)
        cand_s = jnp.concatenate([ts[0][a:a + 1] + ts[1] for a in range(PEER_TOPK)], axis=0)
        cand_i = jnp.concatenate([ti[0][a:a + 1] * N_KEYS + ti[1] for a in range(PEER_TOPK)], axis=0)
        best_s, best_i = _top_rows(cand_s, cand_i, PEER_TOPK)
        e = jnp.exp(best_s - best_s[0:1])
        gate = e / jnp.sum(e, axis=0, keepdims=True)
        idx_ref[h * PEER_TOPK:(h + 1) * PEER_TOPK, :] = best_i
        gate_ref[h * PEER_TOPK:(h + 1) * PEER_TOPK, :] = gate


def peer_topk(q, sub_keys, tn=128):
    n, dq = q.shape
    return pl.pallas_call(
        _peer_topk_kernel,
        grid=(n // tn,),
        in_specs=[pl.BlockSpec((tn, dq), lambda i: (i, 0)),
                  pl.BlockSpec((2, N_KEYS, N_KEYS), lambda i: (0, 0, 0))],
        out_specs=[pl.BlockSpec((PEER_SLOTS, tn), lambda i: (0, i)),
                   pl.BlockSpec((PEER_SLOTS, tn), lambda i: (0, i))],
        out_shape=[jax.ShapeDtypeStruct((PEER_SLOTS, n), I32),
                   jax.ShapeDtypeStruct((PEER_SLOTS, n), F32)],
        compiler_params=_cparams(("parallel",)),
        name="peer_topk",
    )(q, sub_keys)


def pack_expert_table(t):
    e, d = t.shape
    tb = t.astype(BF16).reshape(e, ROW_WORDS, 2, LANES)
    words = lax.bitcast_convert_type(jnp.swapaxes(tb, 2, 3), I32)
    return words.reshape(e * ROW_WORDS, LANES)


def _gather_rows(idx_ref, tbl_ref, stage_ref, n):
    for j in range(PEER_SLOTS):
        start = pl.multiple_of(idx_ref[n, j] * ROW_WORDS, ROW_WORDS)
        stage_ref[j * ROW_WORDS:(j + 1) * ROW_WORDS, :] = tbl_ref[pl.ds(start, ROW_WORDS), :]


def _chunk_mask(d):
    q = lax.broadcasted_iota(I32, (SUBLANES, d), 0)
    c = lax.broadcasted_iota(I32, (SUBLANES, d), 1)
    return (c % SUBLANES == q).astype(F32)


def _peer_act_kernel(idx_ref, x_ref, gate_ref, tbl_ref, rep_t_ref, rep_ref, w_ref, stage_ref, r_ref, *, tb):
    d = PEER_SLOTS * SUBLANES
    mask = _chunk_mask(d)

    def body(n, carry):
        _gather_rows(idx_ref, tbl_ref, stage_ref, n)
        rows = pltpu.bitcast(stage_ref[...], BF16)
        out = _dot_nt(x_ref[n].astype(BF16), rows)
        r_ref[n] = out * mask
        return carry

    lax.fori_loop(0, tb, body, 0)
    part = _split_dot(r_ref[...].reshape(tb * SUBLANES, d), rep_t_ref[...])
    act = jnp.sum(part.reshape(tb, SUBLANES, PEER_SLOTS), axis=1)
    w = gate_ref[...] * _gelu_exact(act)
    w_ref[...] = _dot(w.astype(BF16), rep_ref[...])


def _peer_mix_kernel(idx_ref, w_ref, res_ref, tbl_ref, o_ref, stage_ref, *, tb):
    d = PEER_SLOTS * SUBLANES
    mask = _chunk_mask(LANES)

    def body(n, carry):
        _gather_rows(idx_ref, tbl_ref, stage_ref, n)
        rows = pltpu.bitcast(stage_ref[...], BF16)
        wn = w_ref[n]
        wsel = jnp.concatenate(
            [jnp.broadcast_to(wn[t:t + 1, :], (SUBLANES, LANES)) * mask for t in range(d // LANES)], axis=1)
        o_ref[n] = res_ref[n] + _dot(wsel.astype(BF16), rows)
        return carry

    lax.fori_loop(0, tb, body, 0)


def peer_experts(idx, gate, xn, res, u_tbl, v_tbl, tb=64):
    n, dm = xn.shape
    d = PEER_SLOTS * SUBLANES
    chunks = dm // LANES
    slot_of_col = jnp.arange(d, dtype=I32) // SUBLANES
    rep = (slot_of_col[None, :] == jnp.arange(PEER_SLOTS, dtype=I32)[:, None]).astype(BF16)
    tbl_spec = pl.BlockSpec(u_tbl.shape, lambda i: (0, 0), pipeline_mode=pl.Buffered(1))
    smem_idx = pl.BlockSpec((tb, PEER_SLOTS), lambda i: (i, 0), memory_space=pltpu.SMEM)
    stage = pltpu.VMEM((PEER_SLOTS * ROW_WORDS, LANES), I32)
    w = pl.pallas_call(
        functools.partial(_peer_act_kernel, tb=tb),
        grid=(n // tb,),
        in_specs=[smem_idx,
                  pl.BlockSpec((tb, chunks, LANES), lambda i: (i, 0, 0)),
                  pl.BlockSpec((tb, PEER_SLOTS), lambda i: (i, 0)),
                  tbl_spec,
                  pl.BlockSpec((d, PEER_SLOTS), lambda i: (0, 0)),
                  pl.BlockSpec((PEER_SLOTS, d), lambda i: (0, 0))],
        out_specs=pl.BlockSpec((tb, d), lambda i: (i, 0)),
        out_shape=jax.ShapeDtypeStruct((n, d), F32),
        scratch_shapes=[stage, pltpu.VMEM((tb, SUBLANES, d), F32)],
        compiler_params=_cparams(("arbitrary",)),
        name="peer_act",
    )(idx, xn.reshape(n, chunks, LANES), gate, u_tbl, rep.T, rep)
    out = pl.pallas_call(
        functools.partial(_peer_mix_kernel, tb=tb),
        grid=(n // tb,),
        in_specs=[smem_idx,
                  pl.BlockSpec((tb, d // LANES, LANES), lambda i: (i, 0, 0)),
                  pl.BlockSpec((tb, chunks, LANES), lambda i: (i, 0, 0)),
                  tbl_spec],
        out_specs=pl.BlockSpec((tb, chunks, LANES), lambda i: (i, 0, 0)),
        out_shape=jax.ShapeDtypeStruct((n, chunks, LANES), F32),
        scratch_shapes=[stage],
        compiler_params=_cparams(("arbitrary",)),
        name="peer_mix",
    )(idx, w.reshape(n, d // LANES, LANES), res.reshape(n, chunks, LANES), v_tbl)
    return out.reshape(n, dm)


def peer_ffn(h, norm_g, w_q, sub_keys, u, v):
    q, xn = norm_matmul(h, norm_g, w_q.astype(BF16), with_xn=True)
    idx_t, gate_t = peer_topk(q, sub_keys)
    return peer_experts(idx_t.T, gate_t.T, xn, h, pack_expert_table(u), pack_expert_table(v))


def kernel(x_prompt, x_sample, state_conv, state_ssm, cache_k, cache_v, cache_logf, page_table, meta_tokens,
           norm_mix, norm_ffn, norm_final, gdn_w_in, gdn_conv_w, gdn_a_log, gdn_dt_bias, gdn_out_norm,
           gdn_w_out, fox_w_in, fox_b_f, fox_w_out, peer_w_q, peer_sub_keys, peer_u, peer_v):
    bp, seq, dm = x_prompt.shape
    bs, sn, _ = x_sample.shape
    hd = N_HEADS * HEAD_DIM
    tp = LEAD_PAD + N_META + seq
    first = LEAD_PAD
    n_p = bp * tp
    n_s = bs * sn

    meta = jnp.broadcast_to(meta_tokens[None], (bp, N_META, dm))
    h_p = jnp.concatenate([jnp.zeros((bp, LEAD_PAD, dm), F32), meta, x_prompt], axis=1)
    h = jnp.concatenate([h_p.reshape(n_p, dm), x_sample.reshape(n_s, dm)], axis=0)

    w_in = gdn_w_in[0]
    wg = jnp.pad(w_in, ((0, 0), (0, 4 * hd + LANES - w_in.shape[1]))).astype(BF16)
    z = norm_matmul(h, norm_mix[0], wg)
    zw = z.shape[1]
    z_p = z[:n_p].reshape(bp, tp, zw)
    z_s = jnp.pad(z[n_p:].reshape(bs, sn, zw), ((0, 0), (0, SUBLANES - sn), (0, 0)))
    qkv_p, gb_p = gdn_prep(z_p, z_p, gdn_conv_w[0], gdn_a_log[0], gdn_dt_bias[0],
                           tb=128, zero_first=True, t_lo=first, t_hi=tp)
    o_p, ssm_p = gdn_chunks(qkv_p, gb_p, jnp.zeros((bp, N_HEADS, HEAD_DIM, HEAD_DIM), F32), GDN_CHUNK)
    prev_s = jnp.pad(state_conv[0], ((0, 0), (SUBLANES - (CONV_W - 1), 0), (0, 0)))
    qkv_s, gb_s = gdn_prep(z_s, prev_s, gdn_conv_w[0], gdn_a_log[0], gdn_dt_bias[0],
                           tb=SUBLANES, zero_first=False, t_lo=0, t_hi=sn)
    o_s, ssm_s = gdn_chunks(qkv_s, gb_s, state_ssm[0], SUBLANES)
    o = jnp.concatenate([o_p.reshape(n_p, hd), o_s[:, :sn].reshape(n_s, hd)], axis=0)
    h = gdn_out(o, z, 3, gdn_out_norm[0], gdn_w_out[0].astype(BF16), h)
    new_conv_p = z_p[:, tp - (CONV_W - 1):, :3 * hd][None]
    new_conv_s = z_s[:, sn - (CONV_W - 1):sn, :3 * hd][None]
    h = peer_ffn(h, norm_ffn[0], peer_w_q[0], peer_sub_keys[0], peer_u[0], peer_v[0])

    w_in = fox_w_in[0]
    wf = jnp.pad(w_in, ((0, 0), (0, 3 * hd + LANES - w_in.shape[1]))).astype(BF16)
    z = norm_matmul(h, norm_mix[1], wf)
    zw = z.shape[1]
    z_p = z[:n_p].reshape(bp, tp, zw)
    z_s = z[n_p:].reshape(bs, sn, zw)
    lf_p, cum_p = logf_cumsum(z_p, fox_b_f[0], (3 * hd) // LANES, tb=128)
    o_p = fox_prompt(z_p, cum_p, tq=384, first_key=first)
    z_s8 = jnp.pad(z_s, ((0, 0), (0, SUBLANES - sn), (0, 0)))
    lf_s, _ = logf_cumsum(z_s8, fox_b_f[0], (3 * hd) // LANES, tb=SUBLANES)
    lf_s = lf_s[:, :sn, :N_HEADS]
    lfn_t = jnp.pad(jnp.swapaxes(lf_s, 1, 2), ((0, 0), (0, 0), (0, LANES - sn)))
    n_pool = cache_k.shape[1]
    o_s = fox_sample(z_s, lfn_t, cache_k[0].reshape(n_pool, PAGE, hd), cache_v[0].reshape(n_pool, PAGE, hd),
                     jnp.swapaxes(cache_logf[0], 1, 2), page_table)
    o = jnp.concatenate([o_p.reshape(n_p, hd), o_s.reshape(n_s, hd)], axis=0)
    h = matmul_residual(o, fox_w_out[0].astype(BF16), h)
    kv_p = z_p[:, first:, hd:3 * hd]
    new_k_p = kv_p[..., :hd].reshape(1, bp, tp - first, N_HEADS, HEAD_DIM)
    new_v_p = kv_p[..., hd:].reshape(1, bp, tp - first, N_HEADS, HEAD_DIM)
    new_logf_p = lf_p[:, first:, :N_HEADS][None]
    new_k_s = z_s[..., hd:2 * hd].reshape(1, bs, sn, N_HEADS, HEAD_DIM)
    new_v_s = z_s[..., 2 * hd:3 * hd].reshape(1, bs, sn, N_HEADS, HEAD_DIM)
    new_logf_s = lf_s[None]
    h = peer_ffn(h, norm_ffn[1], peer_w_q[1], peer_sub_keys[1], peer_u[1], peer_v[1])

    y_prompt = final_norm(h[:n_p].reshape(bp, tp, dm), norm_final, (LEAD_PAD + N_META) // 128, seq)
    y_sample = final_norm(h[n_p:].reshape(1, n_s, dm), norm_final, 0, n_s).reshape(bs, sn, dm)
    return (y_prompt, y_sample, new_conv_p, ssm_p[None], new_k_p, new_v_p, new_logf_p,
            new_conv_s, ssm_s[None], new_k_s, new_v_s, new_logf_s)
```

```python
import functools
import math

import jax
import jax.numpy as jnp
from jax import lax
from jax.experimental import pallas as pl
from jax.experimental.pallas import tpu as pltpu

F32 = jnp.float32
BF16 = jnp.bfloat16
I32 = jnp.int32
HI = lax.Precision.HIGHEST

EPS = 1e-6
LANES = 128
SUBLANES = 8
VMEM_LIMIT = 56 * 1024 * 1024

N_META = 16
LEAD_PAD = 112
HEAD_DIM = 128
N_HEADS = 8
CONV_W = 4
GDN_CHUNK = 64
SAMPLE_CHUNK = 16
INV_BLOCK = 16
PAGE = 128
PEER_TOPK = 16
N_KEYS = 128
PEER_SLOTS = N_HEADS * PEER_TOPK
ROW_WORDS = 4
PEER_TOKEN_GROUP = 8
NEG = -1e30


def _exact_div(n, d):
    assert n % d == 0, (n, d)
    return n // d


def _cparams(sem):
    return pltpu.CompilerParams(dimension_semantics=sem, vmem_limit_bytes=VMEM_LIMIT)


def _dot(a, b, precision=None):
    return jnp.dot(a, b, preferred_element_type=F32, precision=precision)


def _dot_nt(a, b, precision=None):
    return lax.dot_general(a, b, (((1,), (1,)), ((), ())), preferred_element_type=F32, precision=precision)


def _dot_tn(a, b, precision=None):
    return lax.dot_general(a, b, (((0,), (0,)), ((), ())), preferred_element_type=F32, precision=precision)


def _split_dot(a, b01):
    hi = a.astype(BF16)
    lo = (a - hi.astype(F32)).astype(BF16)
    return _dot(hi, b01) + _dot(lo, b01)


def _softplus(x):
    return jnp.maximum(x, 0.0) + jnp.log1p(jnp.exp(-jnp.abs(x)))


def _log_sigmoid(x):
    return -_softplus(-x)


def _gelu_exact(x):
    return 0.5 * x * (1.0 + lax.erf(x * (2.0 ** -0.5)))


def _norm_matmul_kernel(x_ref, g_ref, w_ref, o_ref, *xn_ref):
    x = x_ref[...]
    xn = x * lax.rsqrt(jnp.mean(x * x, axis=-1, keepdims=True) + EPS) * g_ref[...]
    if xn_ref:
        xn_ref[0][...] = xn
    o_ref[...] = _dot(xn.astype(BF16), w_ref[...])


def norm_matmul(x, g, w_bf16, tm=256, with_xn=False):
    n, k = x.shape
    m = w_bf16.shape[1]
    out_shape = [jax.ShapeDtypeStruct((n, m), F32)]
    out_specs = [pl.BlockSpec((tm, m), lambda i: (i, 0))]
    if with_xn:
        out_shape.append(jax.ShapeDtypeStruct((n, k), F32))
        out_specs.append(pl.BlockSpec((tm, k), lambda i: (i, 0)))
    res = pl.pallas_call(
        _norm_matmul_kernel,
        grid=(_exact_div(n, tm),),
        in_specs=[pl.BlockSpec((tm, k), lambda i: (i, 0)),
                  pl.BlockSpec((1, k), lambda i: (0, 0)),
                  pl.BlockSpec((k, m), lambda i: (0, 0))],
        out_specs=out_specs,
        out_shape=out_shape,
        compiler_params=_cparams(("parallel",)),
        name="norm_matmul",
    )(x, g.reshape(1, k), w_bf16)
    return res if with_xn else res[0]


def _matmul_res_kernel(x_ref, w_ref, r_ref, o_ref):
    o_ref[...] = r_ref[...] + _dot(x_ref[...].astype(BF16), w_ref[...])


def matmul_residual(x, w_bf16, res, tm=256):
    n, k = x.shape
    m = w_bf16.shape[1]
    return pl.pallas_call(
        _matmul_res_kernel,
        grid=(_exact_div(n, tm),),
        in_specs=[pl.BlockSpec((tm, k), lambda i: (i, 0)),
                  pl.BlockSpec((k, m), lambda i: (0, 0)),
                  pl.BlockSpec((tm, m), lambda i: (i, 0))],
        out_specs=pl.BlockSpec((tm, m), lambda i: (i, 0)),
        out_shape=jax.ShapeDtypeStruct((n, m), F32),
        compiler_params=_cparams(("parallel",)),
        name="matmul_residual",
    )(x, w_bf16, res)


def _gdn_out_kernel(o_ref, gate_ref, nw_ref, w_ref, r_ref, y_ref):
    o = o_ref[...]
    gate = gate_ref[...]
    nw = nw_ref[...]
    parts = []
    for h in range(N_HEADS):
        sl = slice(h * HEAD_DIM, (h + 1) * HEAD_DIM)
        oh = o[:, sl]
        oh = oh * lax.rsqrt(jnp.mean(oh * oh, axis=-1, keepdims=True) + EPS) * nw
        parts.append(oh * jax.nn.silu(gate[:, sl]))
    x = jnp.concatenate(parts, axis=-1)
    y_ref[...] = r_ref[...] + _dot(x.astype(BF16), w_ref[...])


def gdn_out(o, z, gate_col_block, out_norm, w_bf16, res, tm=256):
    n, d = o.shape
    return pl.pallas_call(
        _gdn_out_kernel,
        grid=(_exact_div(n, tm),),
        in_specs=[pl.BlockSpec((tm, d), lambda i: (i, 0)),
                  pl.BlockSpec((tm, d), lambda i: (i, gate_col_block)),
                  pl.BlockSpec((1, HEAD_DIM), lambda i: (0, 0)),
                  pl.BlockSpec((d, d), lambda i: (0, 0)),
                  pl.BlockSpec((tm, d), lambda i: (i, 0))],
        out_specs=pl.BlockSpec((tm, d), lambda i: (i, 0)),
        out_shape=jax.ShapeDtypeStruct((n, d), F32),
        compiler_params=_cparams(("parallel",)),
        name="gdn_out",
    )(o, z, out_norm.reshape(1, HEAD_DIM), w_bf16, res)


def _final_norm_kernel(x_ref, g_ref, o_ref):
    x = x_ref[...]
    o_ref[...] = x * lax.rsqrt(jnp.mean(x * x, axis=-1, keepdims=True) + EPS) * g_ref[...]


def final_norm(x3, g, row_block_offset, t_out, tb=128):
    b, _, d = x3.shape
    return pl.pallas_call(
        _final_norm_kernel,
        grid=(b, t_out // tb),
        in_specs=[pl.BlockSpec((None, tb, d), lambda i, j: (i, j + row_block_offset, 0)),
                  pl.BlockSpec((1, d), lambda i, j: (0, 0))],
        out_specs=pl.BlockSpec((None, tb, d), lambda i, j: (i, j, 0)),
        out_shape=jax.ShapeDtypeStruct((b, t_out, d), F32),
        compiler_params=_cparams(("parallel", "parallel")),
        name="final_norm",
    )(x3, g.reshape(1, d))


def _gdn_prep_kernel(prev_ref, cur_ref, ab_ref, convw_ref, par_ref, qkv_ref, gb_ref, *,
                     tb, zero_first, t_lo, t_hi):
    t = pl.program_id(1)
    cur = cur_ref[...]
    prev = prev_ref[...]
    if zero_first:
        prev = jnp.where(t == 0, 0.0, prev)
    cat = jnp.concatenate([prev, cur], axis=0)
    w = convw_ref[...]
    y = None
    for i in range(CONV_W):
        shift = CONV_W - 1 - i
        xi = cur if shift == 0 else pltpu.roll(cat, shift, axis=0)[SUBLANES:]
        term = xi * w[i:i + 1]
        y = term if y is None else y + term
    y = jax.nn.silu(y)
    hd = N_HEADS * HEAD_DIM
    for h in range(2 * N_HEADS):
        sl = slice(h * HEAD_DIM, (h + 1) * HEAD_DIM)
        xh = y[:, sl]
        xh = xh * lax.rsqrt(jnp.sum(xh * xh, axis=-1, keepdims=True) + EPS)
        if h < N_HEADS:
            xh = xh * (HEAD_DIM ** -0.5)
        qkv_ref[:, sl] = xh
    qkv_ref[:, 2 * hd:] = y[:, 2 * hd:]
    ab = ab_ref[...]
    a_log = par_ref[0:1, :]
    dt_bias = par_ref[1:2, :]
    g = -jnp.exp(a_log) * _softplus(ab + dt_bias)
    beta = jax.nn.sigmoid(ab)
    lane = lax.broadcasted_iota(I32, ab.shape, 1)
    pos = t * tb + lax.broadcasted_iota(I32, ab.shape, 0)
    valid = (pos >= t_lo) & (pos < t_hi)
    gb = jnp.where(lane < N_HEADS, g, jnp.where(lane < 2 * N_HEADS, beta, 0.0))
    gb_ref[...] = jnp.where(valid, gb, 0.0)


def gdn_prep(z3, prev_src, conv_w, a_log, dt_bias, tb, zero_first, t_lo, t_hi):
    b, t, _ = z3.shape
    c = 3 * N_HEADS * HEAD_DIM
    ab_block = (4 * N_HEADS * HEAD_DIM) // LANES
    par = jnp.zeros((SUBLANES, LANES), F32)
    par = par.at[0, :N_HEADS].set(a_log).at[1, :N_HEADS].set(dt_bias)
    if zero_first:
        prev_map = lambda i, j: (i, jnp.maximum(j * (tb // SUBLANES) - 1, 0), 0)
    else:
        prev_map = lambda i, j: (i, 0, 0)
    kern = functools.partial(_gdn_prep_kernel, tb=tb, zero_first=zero_first, t_lo=t_lo, t_hi=t_hi)
    return pl.pallas_call(
        kern,
        grid=(b, t // tb),
        in_specs=[pl.BlockSpec((None, SUBLANES, c), prev_map),
                  pl.BlockSpec((None, tb, c), lambda i, j: (i, j, 0)),
                  pl.BlockSpec((None, tb, LANES), lambda i, j: (i, j, ab_block)),
                  pl.BlockSpec((CONV_W, c), lambda i, j: (0, 0)),
                  pl.BlockSpec((SUBLANES, LANES), lambda i, j: (0, 0))],
        out_specs=[pl.BlockSpec((None, tb, c), lambda i, j: (i, j, 0)),
                   pl.BlockSpec((None, tb, LANES), lambda i, j: (i, j, 0))],
        out_shape=[jax.ShapeDtypeStruct((b, t, c), F32),
                   jax.ShapeDtypeStruct((b, t, LANES), F32)],
        compiler_params=_cparams(("parallel", "arbitrary")),
        name="gdn_prep",
    )(prev_src, z3, z3, conv_w, par)


def _unit_lower_inverses(mats, n):
    ii = lax.broadcasted_iota(I32, (n, n), 0)
    jj = lax.broadcasted_iota(I32, (n, n), 1)
    eye = (ii == jj).astype(F32)
    blk = min(INV_BLOCK, n)
    if n > blk:
        same = (ii // blk) == (jj // blk)
        d = [jnp.where(same, a, 0.0) for a in mats]
        rest = [jnp.where(same, 0.0, a) for a in mats]
    else:
        d = mats
    x = [eye - di for di in d]
    p = d
    width = 1
    while 2 * width < blk:
        p = [_dot(pi, pi, HI) for pi in p]
        x = [xi + _dot(xi, pi, HI) for xi, pi in zip(x, p)]
        width *= 2
    if n == blk:
        return x
    nm = [_dot(xi, ri, HI) for xi, ri in zip(x, rest)]
    y = [eye - ni for ni in nm]
    p = nm
    width = 1
    while 2 * width < n // blk:
        p = [_dot(pi, pi, HI) for pi in p]
        y = [yi + _dot(yi, pi, HI) for yi, pi in zip(y, p)]
        width *= 2
    return [_dot(yi, xi, HI) for yi, xi in zip(y, x)]


def _bdot(a, b):
    return _dot(a.astype(BF16), b.astype(BF16))


def _bdot_nt(a, b):
    return _dot_nt(a.astype(BF16), b.astype(BF16))


def _gdn_chunk_kernel(qkv_ref, gb_ref, gbt_ref, s0_ref, o_ref, sfin_ref, state_ref, *, ln):
    c = pl.program_id(1)

    @pl.when(c == 0)
    def _():
        state_ref[...] = s0_ref[...]

    hd = N_HEADS * HEAD_DIM
    heads = range(N_HEADS)
    ii = lax.broadcasted_iota(I32, (ln, ln), 0)
    jj = lax.broadcasted_iota(I32, (ln, ln), 1)
    tri_l = (ii >= jj).astype(F32)
    gb = gb_ref[...]
    gbt = gbt_ref[...]
    gc_cols = _dot(tri_l, gb, HI)
    gc_rows = _dot_nt(gbt, tri_l, HI)
    causal = ii >= jj
    strict = ii > jj
    q = [qkv_ref[:, h * HEAD_DIM:(h + 1) * HEAD_DIM] for h in heads]
    k = [qkv_ref[:, hd + h * HEAD_DIM:hd + (h + 1) * HEAD_DIM] for h in heads]
    v = [qkv_ref[:, 2 * hd + h * HEAD_DIM:2 * hd + (h + 1) * HEAD_DIM] for h in heads]
    s = [state_ref[h] for h in heads]
    beta = [gb[:, N_HEADS + h:N_HEADS + h + 1] for h in heads]
    gc = [gc_cols[:, h:h + 1] for h in heads]
    decay = [jnp.where(causal, jnp.exp(jnp.where(causal, gc[h] - gc_rows[h:h + 1, :], 0.0)), 0.0) for h in heads]
    kb = [k[h] * beta[h] for h in heads]
    a = [jnp.where(strict, _bdot_nt(kb[h], k[h]) * decay[h], 0.0) for h in heads]
    tinv = _unit_lower_inverses(a, ln)
    egc = [jnp.exp(gc[h]) for h in heads]
    u = [_dot(tinv[h], v[h] * beta[h], HI) for h in heads]
    w = [_dot(tinv[h], kb[h] * egc[h], HI) for h in heads]
    v_new = [u[h] - _bdot(w[h], s[h]) for h in heads]
    attn = [_bdot_nt(q[h], k[h]) * decay[h] for h in heads]
    o = [_bdot(q[h] * egc[h], s[h]) + _bdot(attn[h], v_new[h]) for h in heads]
    g_last = [gc[h][ln - 1:ln, :] for h in heads]
    k_dec = [(k[h] * jnp.exp(g_last[h] - gc[h])).T for h in heads]
    s_new = [s[h] * jnp.exp(g_last[h]) + _bdot(k_dec[h], v_new[h]) for h in heads]
    o_ref[...] = jnp.concatenate(o, axis=-1)
    for h in heads:
        state_ref[h] = s_new[h]

    @pl.when(c == pl.num_programs(1) - 1)
    def _():
        sfin_ref[...] = state_ref[...]


def gdn_chunks(qkv_c, gb, s0, ln):
    b, t, c = qkv_c.shape
    nc = t // ln
    gbt = jnp.swapaxes(gb[..., :2 * N_HEADS].reshape(b, nc, ln, 2 * N_HEADS), 2, 3)
    hd = N_HEADS * HEAD_DIM
    kern = functools.partial(_gdn_chunk_kernel, ln=ln)
    return pl.pallas_call(
        kern,
        grid=(b, nc),
        in_specs=[pl.BlockSpec((None, ln, c), lambda i, j: (i, j, 0)),
                  pl.BlockSpec((None, ln, LANES), lambda i, j: (i, j, 0)),
                  pl.BlockSpec((None, None, 2 * N_HEADS, ln), lambda i, j: (i, j, 0, 0)),
                  pl.BlockSpec((None, N_HEADS, HEAD_DIM, HEAD_DIM), lambda i, j: (i, 0, 0, 0))],
        out_specs=[pl.BlockSpec((None, ln, hd), lambda i, j: (i, j, 0)),
                   pl.BlockSpec((None, N_HEADS, HEAD_DIM, HEAD_DIM), lambda i, j: (i, 0, 0, 0))],
        out_shape=[jax.ShapeDtypeStruct((b, t, hd), F32),
                   jax.ShapeDtypeStruct((b, N_HEADS, HEAD_DIM, HEAD_DIM), F32)],
        scratch_shapes=[pltpu.VMEM((N_HEADS, HEAD_DIM, HEAD_DIM), F32)],
        compiler_params=_cparams(("parallel", "arbitrary")),
        name="gdn_chunks",
    )(qkv_c, gb, gbt, s0)


def _logf_cumsum_kernel(z_ref, bf_ref, lf_ref, cum_ref, carry_ref, *, tb):
    @pl.when(pl.program_id(1) == 0)
    def _():
        carry_ref[...] = jnp.zeros_like(carry_ref)

    lf = _log_sigmoid(z_ref[...] + bf_ref[...])
    lf_ref[...] = lf
    ii = lax.broadcasted_iota(I32, (tb, tb), 0)
    jj = lax.broadcasted_iota(I32, (tb, tb), 1)
    cum = _dot((ii >= jj).astype(F32), lf, HI) + carry_ref[...]
    cum_ref[...] = cum
    carry_ref[...] = cum[tb - 1:tb, :]


def logf_cumsum(z3, b_f, col_block, tb):
    b, t, _ = z3.shape
    bf = jnp.zeros((1, LANES), F32).at[0, :N_HEADS].set(b_f)
    kern = functools.partial(_logf_cumsum_kernel, tb=tb)
    return pl.pallas_call(
        kern,
        grid=(b, t // tb),
        in_specs=[pl.BlockSpec((None, tb, LANES), lambda i, j: (i, j, col_block)),
                  pl.BlockSpec((1, LANES), lambda i, j: (0, 0))],
        out_specs=[pl.BlockSpec((None, tb, LANES), lambda i, j: (i, j, 0)),
                   pl.BlockSpec((None, tb, LANES), lambda i, j: (i, j, 0))],
        out_shape=[jax.ShapeDtypeStruct((b, t, LANES), F32),
                   jax.ShapeDtypeStruct((b, t, LANES), F32)],
        scratch_shapes=[pltpu.VMEM((1, LANES), F32)],
        compiler_params=_cparams(("parallel", "arbitrary")),
        name="logf_cumsum",
    )(z3, bf)


def _fox_prompt_kernel(q_ref, k_ref, v_ref, cq_ref, ck_ref, o_ref, m_ref, l_ref, acc_ref, *, tq, first_key):
    qi = pl.program_id(1)
    ki = pl.program_id(2)

    @pl.when(ki == 0)
    def _():
        m_ref[...] = jnp.full_like(m_ref, NEG)
        l_ref[...] = jnp.zeros_like(l_ref)
        acc_ref[...] = jnp.zeros_like(acc_ref)

    @pl.when(ki <= qi)
    def _():
        qpos = qi * tq + lax.broadcasted_iota(I32, (tq, tq), 0)
        kpos = ki * tq + lax.broadcasted_iota(I32, (tq, tq), 1)
        mask = (qpos >= kpos) & (kpos >= first_key)
        cq = cq_ref[...]
        ck = ck_ref[...]
        for h in range(N_HEADS):
            sl = slice(h * HEAD_DIM, (h + 1) * HEAD_DIM)
            s = _dot_nt(q_ref[:, sl].astype(BF16), k_ref[:, sl].astype(BF16)) * (HEAD_DIM ** -0.5)
            s = s + (cq[:, h:h + 1] - ck[h:h + 1, :])
            s = jnp.where(mask, s, NEG)
            m_old = m_ref[h]
            m_new = jnp.maximum(m_old, jnp.max(s, axis=-1, keepdims=True))
            alpha = jnp.exp(m_old - m_new)
            p = jnp.exp(s - m_new)
            l_ref[h] = alpha * l_ref[h] + jnp.sum(p, axis=-1, keepdims=True)
            acc_ref[:, sl] = alpha * acc_ref[:, sl] + _dot(p.astype(BF16), v_ref[:, sl].astype(BF16))
            m_ref[h] = m_new

    @pl.when(ki == qi)
    def _():
        for h in range(N_HEADS):
            sl = slice(h * HEAD_DIM, (h + 1) * HEAD_DIM)
            o_ref[:, sl] = acc_ref[:, sl] / l_ref[h]


def fox_prompt(z3, cum, tq, first_key):
    b, t, _ = z3.shape
    hd = N_HEADS * HEAD_DIM
    cum_t = jnp.swapaxes(cum[..., :N_HEADS], 1, 2)
    nq = t // tq
    kern = functools.partial(_fox_prompt_kernel, tq=tq, first_key=first_key)
    kmap = lambda col: (lambda i, qi, ki: (i, jnp.minimum(ki, qi), col))
    return pl.pallas_call(
        kern,
        grid=(b, nq, nq),
        in_specs=[pl.BlockSpec((None, tq, hd), lambda i, qi, ki: (i, qi, 0)),
                  pl.BlockSpec((None, tq, hd), kmap(1)),
                  pl.BlockSpec((None, tq, hd), kmap(2)),
                  pl.BlockSpec((None, tq, LANES), lambda i, qi, ki: (i, qi, 0)),
                  pl.BlockSpec((None, N_HEADS, tq), lambda i, qi, ki: (i, 0, jnp.minimum(ki, qi)))],
        out_specs=pl.BlockSpec((None, tq, hd), lambda i, qi, ki: (i, qi, 0)),
        out_shape=jax.ShapeDtypeStruct((b, t, hd), F32),
        scratch_shapes=[pltpu.VMEM((N_HEADS, tq, 1), F32),
                        pltpu.VMEM((N_HEADS, tq, 1), F32),
                        pltpu.VMEM((tq, hd), F32)],
        compiler_params=_cparams(("parallel", "parallel", "arbitrary")),
        name="fox_prompt",
    )(z3, z3, z3, cum, cum_t)


def _lane_cumsum(x):
    lane = lax.broadcasted_iota(I32, x.shape, 1)
    sh = 1
    while sh < LANES:
        x = x + jnp.where(lane >= sh, pltpu.roll(x, sh, axis=1), 0.0)
        sh *= 2
    return x


def _fox_sample_kernel(pt_ref, q_ref, kn_ref, vn_ref, lfn_ref, *rest, sn, pps):
    del pt_ref
    kp_refs, vp_refs, lfp_refs = rest[:pps], rest[pps:2 * pps], rest[2 * pps:3 * pps]
    o_ref, qbd_ref, m_ref, l_ref, acc_ref, carry_ref = rest[3 * pps:]
    p = pl.program_id(1)
    rows = sn * N_HEADS
    hd = N_HEADS * HEAD_DIM
    head_of_col = lax.broadcasted_iota(I32, (N_HEADS, hd), 1) // HEAD_DIM
    head_mask = (head_of_col == lax.broadcasted_iota(I32, (N_HEADS, hd), 0)).astype(F32)

    @pl.when(p == 0)
    def _():
        m_ref[...] = jnp.full_like(m_ref, NEG)
        l_ref[...] = jnp.zeros_like(l_ref)
        acc_ref[...] = jnp.zeros_like(acc_ref)
        carry_ref[...] = jnp.zeros_like(carry_ref)
        q = q_ref[...]
        for s in range(sn):
            qbd_ref[s * N_HEADS:(s + 1) * N_HEADS, :] = (
                jnp.broadcast_to(q[s:s + 1, :], (N_HEADS, hd)) * head_mask)

    def update(scores, vals):
        m_old = m_ref[...]
        m_new = m_old
        for sc in scores:
            m_new = jnp.maximum(m_new, jnp.max(sc, axis=-1, keepdims=True))
        alpha = jnp.exp(m_old - m_new)
        l_new = alpha * l_ref[...]
        acc = alpha * acc_ref[...]
        for sc, vl in zip(scores, vals):
            pr = jnp.exp(sc - m_new)
            l_new = l_new + jnp.sum(pr, axis=-1, keepdims=True)
            acc = acc + _dot(pr.astype(BF16), vl)
        l_ref[...] = l_new
        acc_ref[...] = acc
        m_ref[...] = m_new

    qbd = qbd_ref[...].astype(BF16)
    carry = carry_ref[...]
    scores, vals = [], []
    for r in range(pps):
        cum = _lane_cumsum(lfp_refs[r][...]) + carry
        carry = cum[:, LANES - 1:LANES]
        sc = _dot_nt(qbd, kp_refs[r][...].astype(BF16)) * (HEAD_DIM ** -0.5)
        scores.append(sc - jnp.concatenate([cum] * sn, axis=0))
        vals.append(vp_refs[r][...].astype(BF16))
    update(scores, vals)
    carry_ref[...] = carry

    @pl.when(p == pl.num_programs(1) - 1)
    def _():
        pad = jnp.zeros((PAGE - sn, hd), F32)
        kn = jnp.concatenate([kn_ref[...], pad], axis=0).astype(BF16)
        vn = jnp.concatenate([vn_ref[...], pad], axis=0).astype(BF16)
        cnew = _lane_cumsum(lfn_ref[...]) + carry_ref[...]
        scn = _dot_nt(qbd, kn) * (HEAD_DIM ** -0.5) - jnp.concatenate([cnew] * sn, axis=0)
        qs = lax.broadcasted_iota(I32, (rows, PAGE), 0) // N_HEADS
        kj = lax.broadcasted_iota(I32, (rows, PAGE), 1)
        update([jnp.where(kj <= qs, scn, NEG)], [vn])
        out = acc_ref[...] / l_ref[...]
        out = out * jnp.concatenate([head_mask] * sn, axis=0)
        o_ref[...] = jnp.sum(out.reshape(sn, N_HEADS, hd), axis=1)


def fox_sample(z_s, lfn_t, k_pool, v_pool, lf_pool_t, page_table, pps=4):
    b, sn, _ = z_s.shape
    n_pages = page_table.shape[1]
    hd = N_HEADS * HEAD_DIM
    rows = sn * N_HEADS
    kern = functools.partial(_fox_sample_kernel, sn=sn, pps=pps)
    page_map = lambda r: (lambda i, p, pt: (pt[i, p * pps + r], 0, 0))
    grid_spec = pltpu.PrefetchScalarGridSpec(
        num_scalar_prefetch=1,
        grid=(b, n_pages // pps),
        in_specs=[pl.BlockSpec((None, sn, hd), lambda i, p, pt: (i, 0, 0)),
                  pl.BlockSpec((None, sn, hd), lambda i, p, pt: (i, 0, 1)),
                  pl.BlockSpec((None, sn, hd), lambda i, p, pt: (i, 0, 2)),
                  pl.BlockSpec((None, N_HEADS, LANES), lambda i, p, pt: (i, 0, 0))]
                 + [pl.BlockSpec((None, PAGE, hd), page_map(r)) for r in range(pps)]
                 + [pl.BlockSpec((None, PAGE, hd), page_map(r)) for r in range(pps)]
                 + [pl.BlockSpec((None, N_HEADS, PAGE), page_map(r)) for r in range(pps)],
        out_specs=pl.BlockSpec((None, sn, hd), lambda i, p, pt: (i, 0, 0)),
        scratch_shapes=[pltpu.VMEM((rows, hd), F32),
                        pltpu.VMEM((rows, 1), F32),
                        pltpu.VMEM((rows, 1), F32),
                        pltpu.VMEM((rows, hd), F32),
                        pltpu.VMEM((N_HEADS, 1), F32)],
    )
    return pl.pallas_call(
        kern,
        grid_spec=grid_spec,
        out_shape=jax.ShapeDtypeStruct((b, sn, hd), F32),
        compiler_params=_cparams(("parallel", "arbitrary")),
        name="fox_sample",
    )(page_table, z_s, z_s, z_s, lfn_t, *([k_pool] * pps), *([v_pool] * pps), *([lf_pool_t] * pps))


def _top_rows(x, payload, k):
    r = x.shape[0]
    row = lax.broadcasted_iota(I32, x.shape, 0)
    vals, picks = [], []
    for _ in range(k):
        m = jnp.max(x, axis=0, keepdims=True)
        pos = jnp.min(jnp.where(x == m, row, r), axis=0, keepdims=True)
        sel = row == pos
        vals.append(m)
        picks.append(pos if payload is None else jnp.max(jnp.where(sel, payload, -1), axis=0, keepdims=True))
        x = jnp.where(sel, -jnp.inf, x)
    return jnp.concatenate(vals, axis=0), jnp.concatenate(picks, axis=0)


def _peer_topk_kernel(q_ref, keys_ref, idx_ref, gate_ref):
    for h in range(N_HEADS):
        ts, ti = [], []
        for half in range(2):
            col = (2 * h + half) * N_KEYS
            sc = _dot_nt(keys_ref[half], q_ref[:, col:col + N_KEYS], HI)
            v, i = _top_rows(sc, None, PEER_TOPK)
            ts.append(v)
            ti.append(i)
        half_k = PEER_TOPK // 2
        cand_s = [ts[0][0:1] + ts[1]]
        cand_i = [ti[0][0:1] * N_KEYS + ti[1]]
        for a in range(1, half_k):
            cand_s.append(ts[0][a:a + 1] + ts[1][:half_k])
            cand_i.append(ti[0][a:a + 1] * N_KEYS + ti[1][:half_k])
        cand_s.append(ts[0][half_k:] + ts[1][0:1])
        cand_i.append(ti[0][half_k:] * N_KEYS + ti[1][0:1])
        best_s, best_i = _top_rows(jnp.concatenate(cand_s, axis=0), jnp.concatenate(cand_i, axis=0), PEER_TOPK)
        e = jnp.exp(best_s - best_s[0:1])
        gate = e / jnp.sum(e, axis=0, keepdims=True)
        idx_ref[h * PEER_TOPK:(h + 1) * PEER_TOPK, :] = best_i
        gate_ref[h * PEER_TOPK:(h + 1) * PEER_TOPK, :] = gate


def peer_topk(q, sub_keys, tn=128):
    n, dq = q.shape
    return pl.pallas_call(
        _peer_topk_kernel,
        grid=(_exact_div(n, tn),),
        in_specs=[pl.BlockSpec((tn, dq), lambda i: (i, 0)),
                  pl.BlockSpec((2, N_KEYS, N_KEYS), lambda i: (0, 0, 0))],
        out_specs=[pl.BlockSpec((PEER_SLOTS, tn), lambda i: (0, i)),
                   pl.BlockSpec((PEER_SLOTS, tn), lambda i: (0, i))],
        out_shape=[jax.ShapeDtypeStruct((PEER_SLOTS, n), I32),
                   jax.ShapeDtypeStruct((PEER_SLOTS, n), F32)],
        compiler_params=_cparams(("parallel",)),
        name="peer_topk",
    )(q, sub_keys)


def pack_expert_table(t):
    e, d = t.shape
    tb = t.astype(BF16).reshape(e, ROW_WORDS, 2, LANES)
    words = lax.bitcast_convert_type(jnp.swapaxes(tb, 2, 3), I32)
    return words.reshape(e * ROW_WORDS, LANES)


def _gather_rows(row_ref, tbl_ref, stage_ref, n):
    offs = row_ref.at[n]
    for j in range(PEER_SLOTS):
        start = pl.multiple_of(offs[j], ROW_WORDS)
        stage_ref[j * ROW_WORDS:(j + 1) * ROW_WORDS, :] = tbl_ref[pl.ds(start, ROW_WORDS), :]


def _chunk_mask(d):
    q = lax.broadcasted_iota(I32, (SUBLANES, d), 0)
    c = lax.broadcasted_iota(I32, (SUBLANES, d), 1)
    return (c % SUBLANES == q).astype(F32)


def _for_token_groups(tb, idx_ref, tbl_ref, stage_refs, consume):
    group = len(stage_refs)

    def body(i, carry):
        base = i * group
        _gather_rows(idx_ref, tbl_ref, stage_refs[0], base)
        for t in range(group):
            if t + 1 < group:
                _gather_rows(idx_ref, tbl_ref, stage_refs[t + 1], base + t + 1)
            consume(base + t, stage_refs[t])
        return carry

    lax.fori_loop(0, _exact_div(tb, group), body, 0)


def _peer_act_kernel(idx_ref, x_ref, gate_ref, tbl_ref, rep_t_ref, rep_ref, w_ref, r_ref, *stage_refs, tb):
    d = PEER_SLOTS * SUBLANES
    mask = _chunk_mask(d)

    def consume(n, stage_ref):
        rows = pltpu.bitcast(stage_ref[...], BF16)
        out = _dot_nt(x_ref[n].astype(BF16), rows)
        r_ref[n] = out * mask

    _for_token_groups(tb, idx_ref, tbl_ref, stage_refs, consume)
    part = _split_dot(r_ref[...].reshape(tb * SUBLANES, d), rep_t_ref[...])
    act = jnp.sum(part.reshape(tb, SUBLANES, PEER_SLOTS), axis=1)
    w = gate_ref[...] * _gelu_exact(act)
    w_ref[...] = _dot(w.astype(BF16), rep_ref[...])


def _peer_mix_kernel(idx_ref, w_ref, res_ref, tbl_ref, o_ref, *stage_refs, tb):
    d = PEER_SLOTS * SUBLANES
    mask = _chunk_mask(LANES)

    def consume(n, stage_ref):
        rows = pltpu.bitcast(stage_ref[...], BF16)
        wn = w_ref[n]
        wsel = jnp.concatenate(
            [jnp.broadcast_to(wn[t:t + 1, :], (SUBLANES, LANES)) * mask for t in range(d // LANES)], axis=1)
        o_ref[n] = res_ref[n] + _dot(wsel.astype(BF16), rows)

    _for_token_groups(tb, idx_ref, tbl_ref, stage_refs, consume)


def peer_experts(idx, gate, xn, res, u_tbl, v_tbl, tb=64):
    n, dm = xn.shape
    d = PEER_SLOTS * SUBLANES
    chunks = dm // LANES
    slot_of_col = jnp.arange(d, dtype=I32) // SUBLANES
    rep = (slot_of_col[None, :] == jnp.arange(PEER_SLOTS, dtype=I32)[:, None]).astype(BF16)
    tbl_spec = pl.BlockSpec(u_tbl.shape, lambda i: (0, 0), pipeline_mode=pl.Buffered(1))
    smem_idx = pl.BlockSpec((tb, PEER_SLOTS), lambda i: (i, 0), memory_space=pltpu.SMEM)
    stages = [pltpu.VMEM((PEER_SLOTS * ROW_WORDS, LANES), I32)] * PEER_TOKEN_GROUP
    w = pl.pallas_call(
        functools.partial(_peer_act_kernel, tb=tb),
        grid=(_exact_div(n, tb),),
        in_specs=[smem_idx,
                  pl.BlockSpec((tb, chunks, LANES), lambda i: (i, 0, 0)),
                  pl.BlockSpec((tb, PEER_SLOTS), lambda i: (i, 0)),
                  tbl_spec,
                  pl.BlockSpec((d, PEER_SLOTS), lambda i: (0, 0)),
                  pl.BlockSpec((PEER_SLOTS, d), lambda i: (0, 0))],
        out_specs=pl.BlockSpec((tb, d), lambda i: (i, 0)),
        out_shape=jax.ShapeDtypeStruct((n, d), F32),
        scratch_shapes=[pltpu.VMEM((tb, SUBLANES, d), F32)] + stages,
        compiler_params=_cparams(("arbitrary",)),
        name="peer_act",
    )(idx, xn.reshape(n, chunks, LANES), gate, u_tbl, rep.T, rep)
    out = pl.pallas_call(
        functools.partial(_peer_mix_kernel, tb=tb),
        grid=(_exact_div(n, tb),),
        in_specs=[smem_idx,
                  pl.BlockSpec((tb, d // LANES, LANES), lambda i: (i, 0, 0)),
                  pl.BlockSpec((tb, chunks, LANES), lambda i: (i, 0, 0)),
                  tbl_spec],
        out_specs=pl.BlockSpec((tb, chunks, LANES), lambda i: (i, 0, 0)),
        out_shape=jax.ShapeDtypeStruct((n, chunks, LANES), F32),
        scratch_shapes=stages,
        compiler_params=_cparams(("arbitrary",)),
        name="peer_mix",
    )(idx, w.reshape(n, d // LANES, LANES), res.reshape(n, chunks, LANES), v_tbl)
    return out.reshape(n, dm)


def peer_ffn(h, norm_g, w_q, sub_keys, u, v):
    q, xn = norm_matmul(h, norm_g, w_q.astype(BF16), with_xn=True)
    idx_t, gate_t = peer_topk(q, sub_keys)
    return peer_experts(idx_t.T * ROW_WORDS, gate_t.T, xn, h, pack_expert_table(u), pack_expert_table(v))


def kernel(x_prompt, x_sample, state_conv, state_ssm, cache_k, cache_v, cache_logf, page_table, meta_tokens,
           norm_mix, norm_ffn, norm_final, gdn_w_in, gdn_conv_w, gdn_a_log, gdn_dt_bias, gdn_out_norm,
           gdn_w_out, fox_w_in, fox_b_f, fox_w_out, peer_w_q, peer_sub_keys, peer_u, peer_v):
    bp, seq, dm = x_prompt.shape
    bs, sn, _ = x_sample.shape
    hd = N_HEADS * HEAD_DIM
    tp = LEAD_PAD + N_META + seq
    first = LEAD_PAD
    n_p = bp * tp
    n_s = bs * sn

    meta = jnp.broadcast_to(meta_tokens[None], (bp, N_META, dm))
    h_p = jnp.concatenate([jnp.zeros((bp, LEAD_PAD, dm), F32), meta, x_prompt], axis=1)
    h = jnp.concatenate([h_p.reshape(n_p, dm), x_sample.reshape(n_s, dm)], axis=0)

    w_in = gdn_w_in[0]
    wg = jnp.pad(w_in, ((0, 0), (0, 4 * hd + LANES - w_in.shape[1]))).astype(BF16)
    z = norm_matmul(h, norm_mix[0], wg)
    zw = z.shape[1]
    z_p = z[:n_p].reshape(bp, tp, zw)
    z_s = jnp.pad(z[n_p:].reshape(bs, sn, zw), ((0, 0), (0, SAMPLE_CHUNK - sn), (0, 0)))
    qkv_p, gb_p = gdn_prep(z_p, z_p, gdn_conv_w[0], gdn_a_log[0], gdn_dt_bias[0],
                           tb=128, zero_first=True, t_lo=first, t_hi=tp)
    o_p, ssm_p = gdn_chunks(qkv_p, gb_p, jnp.zeros((bp, N_HEADS, HEAD_DIM, HEAD_DIM), F32), GDN_CHUNK)
    prev_s = jnp.pad(state_conv[0], ((0, 0), (SUBLANES - (CONV_W - 1), 0), (0, 0)))
    qkv_s, gb_s = gdn_prep(z_s, prev_s, gdn_conv_w[0], gdn_a_log[0], gdn_dt_bias[0],
                           tb=SAMPLE_CHUNK, zero_first=False, t_lo=0, t_hi=sn)
    o_s, ssm_s = gdn_chunks(qkv_s, gb_s, state_ssm[0], SAMPLE_CHUNK)
    o = jnp.concatenate([o_p.reshape(n_p, hd), o_s[:, :sn].reshape(n_s, hd)], axis=0)
    h = gdn_out(o, z, 3, gdn_out_norm[0], gdn_w_out[0].astype(BF16), h)
    new_conv_p = z_p[:, tp - (CONV_W - 1):, :3 * hd][None]
    new_conv_s = z_s[:, sn - (CONV_W - 1):sn, :3 * hd][None]
    h = peer_ffn(h, norm_ffn[0], peer_w_q[0], peer_sub_keys[0], peer_u[0], peer_v[0])

    w_in = fox_w_in[0]
    wf = jnp.pad(w_in, ((0, 0), (0, 3 * hd + LANES - w_in.shape[1]))).astype(BF16)
    z = norm_matmul(h, norm_mix[1], wf)
    zw = z.shape[1]
    z_p = z[:n_p].reshape(bp, tp, zw)
    z_s = z[n_p:].reshape(bs, sn, zw)
    lf_p, cum_p = logf_cumsum(z_p, fox_b_f[0], (3 * hd) // LANES, tb=128)
    o_p = fox_prompt(z_p, cum_p, tq=384, first_key=first)
    z_s8 = jnp.pad(z_s, ((0, 0), (0, SUBLANES - sn), (0, 0)))
    lf_s, _ = logf_cumsum(z_s8, fox_b_f[0], (3 * hd) // LANES, tb=SUBLANES)
    lf_s = lf_s[:, :sn, :N_HEADS]
    lfn_t = jnp.pad(jnp.swapaxes(lf_s, 1, 2), ((0, 0), (0, 0), (0, LANES - sn)))
    o_s = fox_sample(z_s, lfn_t, cache_k.reshape(-1, PAGE, hd), cache_v.reshape(-1, PAGE, hd),
                     jnp.swapaxes(cache_logf.reshape(-1, PAGE, N_HEADS), 1, 2), page_table)
    o = jnp.concatenate([o_p.reshape(n_p, hd), o_s.reshape(n_s, hd)], axis=0)
    h = matmul_residual(o, fox_w_out[0].astype(BF16), h)
    kv_p = z_p[:, first:, hd:3 * hd]
    new_k_p = kv_p[..., :hd].reshape(1, bp, tp - first, N_HEADS, HEAD_DIM)
    new_v_p = kv_p[..., hd:].reshape(1, bp, tp - first, N_HEADS, HEAD_DIM)
    new_logf_p = lf_p[:, first:, :N_HEADS][None]
    new_k_s = z_s[..., hd:2 * hd].reshape(1, bs, sn, N_HEADS, HEAD_DIM)
    new_v_s = z_s[..., 2 * hd:3 * hd].reshape(1, bs, sn, N_HEADS, HEAD_DIM)
    new_logf_s = lf_s[None]
    h = peer_ffn(h, norm_ffn[1], peer_w_q[1], peer_sub_keys[1], peer_u[1], peer_v[1])

    y_prompt = final_norm(h[:n_p].reshape(bp, tp, dm), norm_final, (LEAD_PAD + N_META) // 128, seq)
    y_sample = final_norm(h[n_p:].reshape(1, n_s, dm), norm_final, 0, n_s).reshape(bs, sn, dm)
    return (y_prompt, y_sample, new_conv_p, ssm_p[None], new_k_p, new_v_p, new_logf_p,
            new_conv_s, ssm_s[None], new_k_s, new_v_s, new_logf_s)
```

```python
import functools
import math

import jax
import jax.numpy as jnp
from jax import lax
from jax.experimental import pallas as pl
from jax.experimental.pallas import tpu as pltpu

F32 = jnp.float32
BF16 = jnp.bfloat16
I32 = jnp.int32
HI = lax.Precision.HIGHEST

EPS = 1e-6
LANES = 128
SUBLANES = 8
VMEM_LIMIT = 56 * 1024 * 1024

N_META = 16
LEAD_PAD = 112
HEAD_DIM = 128
N_HEADS = 8
CONV_W = 4
GDN_CHUNK = 64
SAMPLE_CHUNK = 16
INV_BLOCK = 16
PAGE = 128
PEER_TOPK = 16
N_KEYS = 128
PEER_SLOTS = N_HEADS * PEER_TOPK
ROW_WORDS = 4
PEER_TOKEN_GROUP = 8
NEG = -1e30
LOG2E = 1.4426950408889634


def _exact_div(n, d):
    assert n % d == 0, (n, d)
    return n // d


def _cparams(sem):
    return pltpu.CompilerParams(dimension_semantics=sem, vmem_limit_bytes=VMEM_LIMIT)


def _dot(a, b, precision=None):
    return jnp.dot(a, b, preferred_element_type=F32, precision=precision)


def _dot_nt(a, b, precision=None):
    return lax.dot_general(a, b, (((1,), (1,)), ((), ())), preferred_element_type=F32, precision=precision)


def _dot_tn(a, b, precision=None):
    return lax.dot_general(a, b, (((0,), (0,)), ((), ())), preferred_element_type=F32, precision=precision)


def _split_dot(a, b01):
    hi = a.astype(BF16)
    lo = (a - hi.astype(F32)).astype(BF16)
    return _dot(hi, b01) + _dot(lo, b01)


def _softplus(x):
    return jnp.maximum(x, 0.0) + jnp.log1p(jnp.exp(-jnp.abs(x)))


def _log_sigmoid(x):
    return -_softplus(-x)


def _gelu_exact(x):
    return 0.5 * x * (1.0 + lax.erf(x * (2.0 ** -0.5)))


def _norm_matmul_kernel(x_ref, g_ref, w_ref, o_ref, *xn_ref):
    x = x_ref[...]
    xn = x * lax.rsqrt(jnp.mean(x * x, axis=-1, keepdims=True) + EPS) * g_ref[...]
    if xn_ref:
        xn_ref[0][...] = xn
    o_ref[...] = _dot(xn.astype(BF16), w_ref[...])


def norm_matmul(x, g, w_bf16, tm=256, with_xn=False):
    n, k = x.shape
    m = w_bf16.shape[1]
    out_shape = [jax.ShapeDtypeStruct((n, m), F32)]
    out_specs = [pl.BlockSpec((tm, m), lambda i: (i, 0))]
    if with_xn:
        out_shape.append(jax.ShapeDtypeStruct((n, k), F32))
        out_specs.append(pl.BlockSpec((tm, k), lambda i: (i, 0)))
    res = pl.pallas_call(
        _norm_matmul_kernel,
        grid=(_exact_div(n, tm),),
        in_specs=[pl.BlockSpec((tm, k), lambda i: (i, 0)),
                  pl.BlockSpec((1, k), lambda i: (0, 0)),
                  pl.BlockSpec((k, m), lambda i: (0, 0))],
        out_specs=out_specs,
        out_shape=out_shape,
        compiler_params=_cparams(("parallel",)),
        name="norm_matmul",
    )(x, g.reshape(1, k), w_bf16)
    return res if with_xn else res[0]


def _matmul_res_kernel(x_ref, w_ref, r_ref, o_ref):
    o_ref[...] = r_ref[...] + _dot(x_ref[...].astype(BF16), w_ref[...])


def matmul_residual(x, w_bf16, res, tm=256):
    n, k = x.shape
    m = w_bf16.shape[1]
    return pl.pallas_call(
        _matmul_res_kernel,
        grid=(_exact_div(n, tm),),
        in_specs=[pl.BlockSpec((tm, k), lambda i: (i, 0)),
                  pl.BlockSpec((k, m), lambda i: (0, 0)),
                  pl.BlockSpec((tm, m), lambda i: (i, 0))],
        out_specs=pl.BlockSpec((tm, m), lambda i: (i, 0)),
        out_shape=jax.ShapeDtypeStruct((n, m), F32),
        compiler_params=_cparams(("parallel",)),
        name="matmul_residual",
    )(x, w_bf16, res)


def _gdn_out_kernel(o_ref, gate_ref, nw_ref, w_ref, r_ref, y_ref):
    o = o_ref[...]
    gate = gate_ref[...]
    nw = nw_ref[...]
    parts = []
    for h in range(N_HEADS):
        sl = slice(h * HEAD_DIM, (h + 1) * HEAD_DIM)
        oh = o[:, sl]
        oh = oh * lax.rsqrt(jnp.mean(oh * oh, axis=-1, keepdims=True) + EPS) * nw
        parts.append(oh * jax.nn.silu(gate[:, sl]))
    x = jnp.concatenate(parts, axis=-1)
    y_ref[...] = r_ref[...] + _dot(x.astype(BF16), w_ref[...])


def gdn_out(o, z, gate_col_block, out_norm, w_bf16, res, tm=256):
    n, d = o.shape
    return pl.pallas_call(
        _gdn_out_kernel,
        grid=(_exact_div(n, tm),),
        in_specs=[pl.BlockSpec((tm, d), lambda i: (i, 0)),
                  pl.BlockSpec((tm, d), lambda i: (i, gate_col_block)),
                  pl.BlockSpec((1, HEAD_DIM), lambda i: (0, 0)),
                  pl.BlockSpec((d, d), lambda i: (0, 0)),
                  pl.BlockSpec((tm, d), lambda i: (i, 0))],
        out_specs=pl.BlockSpec((tm, d), lambda i: (i, 0)),
        out_shape=jax.ShapeDtypeStruct((n, d), F32),
        compiler_params=_cparams(("parallel",)),
        name="gdn_out",
    )(o, z, out_norm.reshape(1, HEAD_DIM), w_bf16, res)


def _final_norm_kernel(x_ref, g_ref, o_ref):
    x = x_ref[...]
    o_ref[...] = x * lax.rsqrt(jnp.mean(x * x, axis=-1, keepdims=True) + EPS) * g_ref[...]


def final_norm(x3, g, row_block_offset, t_out, tb=128):
    b, _, d = x3.shape
    return pl.pallas_call(
        _final_norm_kernel,
        grid=(b, t_out // tb),
        in_specs=[pl.BlockSpec((None, tb, d), lambda i, j: (i, j + row_block_offset, 0)),
                  pl.BlockSpec((1, d), lambda i, j: (0, 0))],
        out_specs=pl.BlockSpec((None, tb, d), lambda i, j: (i, j, 0)),
        out_shape=jax.ShapeDtypeStruct((b, t_out, d), F32),
        compiler_params=_cparams(("parallel", "parallel")),
        name="final_norm",
    )(x3, g.reshape(1, d))


def _gdn_prep_kernel(prev_ref, cur_ref, ab_ref, convw_ref, par_ref, qkv_ref, gb_ref, *,
                     tb, zero_first, t_lo, t_hi):
    t = pl.program_id(1)
    cur = cur_ref[...]
    prev = prev_ref[...]
    if zero_first:
        prev = jnp.where(t == 0, 0.0, prev)
    cat = jnp.concatenate([prev, cur], axis=0)
    w = convw_ref[...]
    y = None
    for i in range(CONV_W):
        shift = CONV_W - 1 - i
        xi = cur if shift == 0 else pltpu.roll(cat, shift, axis=0)[SUBLANES:]
        term = xi * w[i:i + 1]
        y = term if y is None else y + term
    y = jax.nn.silu(y)
    hd = N_HEADS * HEAD_DIM
    for h in range(2 * N_HEADS):
        sl = slice(h * HEAD_DIM, (h + 1) * HEAD_DIM)
        xh = y[:, sl]
        xh = xh * lax.rsqrt(jnp.sum(xh * xh, axis=-1, keepdims=True) + EPS)
        if h < N_HEADS:
            xh = xh * (HEAD_DIM ** -0.5)
        qkv_ref[:, sl] = xh
    qkv_ref[:, 2 * hd:] = y[:, 2 * hd:]
    ab = ab_ref[...]
    a_log = par_ref[0:1, :]
    dt_bias = par_ref[1:2, :]
    g = -jnp.exp(a_log) * _softplus(ab + dt_bias)
    beta = jax.nn.sigmoid(ab)
    lane = lax.broadcasted_iota(I32, ab.shape, 1)
    pos = t * tb + lax.broadcasted_iota(I32, ab.shape, 0)
    valid = (pos >= t_lo) & (pos < t_hi)
    gb = jnp.where(lane < N_HEADS, g, jnp.where(lane < 2 * N_HEADS, beta, 0.0))
    gb_ref[...] = jnp.where(valid, gb, 0.0)


def gdn_prep(z3, prev_src, conv_w, a_log, dt_bias, tb, zero_first, t_lo, t_hi):
    b, t, _ = z3.shape
    c = 3 * N_HEADS * HEAD_DIM
    ab_block = (4 * N_HEADS * HEAD_DIM) // LANES
    par = jnp.zeros((SUBLANES, LANES), F32)
    par = par.at[0, :N_HEADS].set(a_log).at[1, :N_HEADS].set(dt_bias)
    if zero_first:
        prev_map = lambda i, j: (i, jnp.maximum(j * (tb // SUBLANES) - 1, 0), 0)
    else:
        prev_map = lambda i, j: (i, 0, 0)
    kern = functools.partial(_gdn_prep_kernel, tb=tb, zero_first=zero_first, t_lo=t_lo, t_hi=t_hi)
    return pl.pallas_call(
        kern,
        grid=(b, t // tb),
        in_specs=[pl.BlockSpec((None, SUBLANES, c), prev_map),
                  pl.BlockSpec((None, tb, c), lambda i, j: (i, j, 0)),
                  pl.BlockSpec((None, tb, LANES), lambda i, j: (i, j, ab_block)),
                  pl.BlockSpec((CONV_W, c), lambda i, j: (0, 0)),
                  pl.BlockSpec((SUBLANES, LANES), lambda i, j: (0, 0))],
        out_specs=[pl.BlockSpec((None, tb, c), lambda i, j: (i, j, 0)),
                   pl.BlockSpec((None, tb, LANES), lambda i, j: (i, j, 0))],
        out_shape=[jax.ShapeDtypeStruct((b, t, c), F32),
                   jax.ShapeDtypeStruct((b, t, LANES), F32)],
        compiler_params=_cparams(("parallel", "arbitrary")),
        name="gdn_prep",
    )(prev_src, z3, z3, conv_w, par)


def _unit_lower_inverses(mats, n):
    ii = lax.broadcasted_iota(I32, (n, n), 0)
    jj = lax.broadcasted_iota(I32, (n, n), 1)
    eye = (ii == jj).astype(F32)
    blk = min(INV_BLOCK, n)
    if n > blk:
        same = (ii // blk) == (jj // blk)
        d = [jnp.where(same, a, 0.0) for a in mats]
        rest = [jnp.where(same, 0.0, a) for a in mats]
    else:
        d = mats
    x = [eye - di for di in d]
    p = d
    width = 1
    while 2 * width < blk:
        p = [_dot(pi, pi, HI) for pi in p]
        x = [xi + _dot(xi, pi, HI) for xi, pi in zip(x, p)]
        width *= 2
    if n == blk:
        return x
    nm = [_dot(xi, ri, HI) for xi, ri in zip(x, rest)]
    y = [eye - ni for ni in nm]
    p = nm
    width = 1
    while 2 * width < n // blk:
        p = [_dot(pi, pi, HI) for pi in p]
        y = [yi + _dot(yi, pi, HI) for yi, pi in zip(y, p)]
        width *= 2
    return [_dot(yi, xi, HI) for yi, xi in zip(y, x)]


def _bdot(a, b):
    return _dot(a.astype(BF16), b.astype(BF16))


def _bdot_nt(a, b):
    return _dot_nt(a.astype(BF16), b.astype(BF16))


def _gdn_chunk_kernel(qkv_ref, gb_ref, gbt_ref, s0_ref, o_ref, sfin_ref, state_ref, *, ln):
    c = pl.program_id(1)

    @pl.when(c == 0)
    def _():
        state_ref[...] = s0_ref[...]

    hd = N_HEADS * HEAD_DIM
    heads = range(N_HEADS)
    ii = lax.broadcasted_iota(I32, (ln, ln), 0)
    jj = lax.broadcasted_iota(I32, (ln, ln), 1)
    tri_l = (ii >= jj).astype(F32)
    gb = gb_ref[...]
    gbt = gbt_ref[...]
    gc_cols = _dot(tri_l, gb, HI)
    gc_rows = _dot_nt(gbt, tri_l, HI)
    causal = ii >= jj
    strict = ii > jj
    q = [qkv_ref[:, h * HEAD_DIM:(h + 1) * HEAD_DIM] for h in heads]
    k = [qkv_ref[:, hd + h * HEAD_DIM:hd + (h + 1) * HEAD_DIM] for h in heads]
    v = [qkv_ref[:, 2 * hd + h * HEAD_DIM:2 * hd + (h + 1) * HEAD_DIM] for h in heads]
    s = [state_ref[h] for h in heads]
    beta = [gb[:, N_HEADS + h:N_HEADS + h + 1] for h in heads]
    gc = [gc_cols[:, h:h + 1] for h in heads]
    decay = [jnp.where(causal, jnp.exp(jnp.where(causal, gc[h] - gc_rows[h:h + 1, :], 0.0)), 0.0) for h in heads]
    kb = [k[h] * beta[h] for h in heads]
    a = [jnp.where(strict, _bdot_nt(kb[h], k[h]) * decay[h], 0.0) for h in heads]
    tinv = _unit_lower_inverses(a, ln)
    egc = [jnp.exp(gc[h]) for h in heads]
    u = [_dot(tinv[h], v[h] * beta[h], HI) for h in heads]
    w = [_dot(tinv[h], kb[h] * egc[h], HI) for h in heads]
    v_new = [u[h] - _bdot(w[h], s[h]) for h in heads]
    attn = [_bdot_nt(q[h], k[h]) * decay[h] for h in heads]
    o = [_bdot(q[h] * egc[h], s[h]) + _bdot(attn[h], v_new[h]) for h in heads]
    g_last = [gc[h][ln - 1:ln, :] for h in heads]
    k_dec = [(k[h] * jnp.exp(g_last[h] - gc[h])).T for h in heads]
    s_new = [s[h] * jnp.exp(g_last[h]) + _bdot(k_dec[h], v_new[h]) for h in heads]
    o_ref[...] = jnp.concatenate(o, axis=-1)
    for h in heads:
        state_ref[h] = s_new[h]

    @pl.when(c == pl.num_programs(1) - 1)
    def _():
        sfin_ref[...] = state_ref[...]


def gdn_chunks(qkv_c, gb, s0, ln):
    b, t, c = qkv_c.shape
    nc = t // ln
    gbt = jnp.swapaxes(gb[..., :2 * N_HEADS].reshape(b, nc, ln, 2 * N_HEADS), 2, 3)
    hd = N_HEADS * HEAD_DIM
    kern = functools.partial(_gdn_chunk_kernel, ln=ln)
    return pl.pallas_call(
        kern,
        grid=(b, nc),
        in_specs=[pl.BlockSpec((None, ln, c), lambda i, j: (i, j, 0)),
                  pl.BlockSpec((None, ln, LANES), lambda i, j: (i, j, 0)),
                  pl.BlockSpec((None, None, 2 * N_HEADS, ln), lambda i, j: (i, j, 0, 0)),
                  pl.BlockSpec((None, N_HEADS, HEAD_DIM, HEAD_DIM), lambda i, j: (i, 0, 0, 0))],
        out_specs=[pl.BlockSpec((None, ln, hd), lambda i, j: (i, j, 0)),
                   pl.BlockSpec((None, N_HEADS, HEAD_DIM, HEAD_DIM), lambda i, j: (i, 0, 0, 0))],
        out_shape=[jax.ShapeDtypeStruct((b, t, hd), F32),
                   jax.ShapeDtypeStruct((b, N_HEADS, HEAD_DIM, HEAD_DIM), F32)],
        scratch_shapes=[pltpu.VMEM((N_HEADS, HEAD_DIM, HEAD_DIM), F32)],
        compiler_params=_cparams(("parallel", "arbitrary")),
        name="gdn_chunks",
    )(qkv_c, gb, gbt, s0)


def _logf_cumsum_kernel(z_ref, bf_ref, lf_ref, cum_ref, carry_ref, *, tb):
    @pl.when(pl.program_id(1) == 0)
    def _():
        carry_ref[...] = jnp.zeros_like(carry_ref)

    lf = _log_sigmoid(z_ref[...] + bf_ref[...])
    lf_ref[...] = lf
    ii = lax.broadcasted_iota(I32, (tb, tb), 0)
    jj = lax.broadcasted_iota(I32, (tb, tb), 1)
    cum = _dot((ii >= jj).astype(F32), lf, HI) + carry_ref[...]
    cum_ref[...] = cum
    carry_ref[...] = cum[tb - 1:tb, :]


def logf_cumsum(z3, b_f, col_block, tb):
    b, t, _ = z3.shape
    bf = jnp.zeros((1, LANES), F32).at[0, :N_HEADS].set(b_f)
    kern = functools.partial(_logf_cumsum_kernel, tb=tb)
    return pl.pallas_call(
        kern,
        grid=(b, t // tb),
        in_specs=[pl.BlockSpec((None, tb, LANES), lambda i, j: (i, j, col_block)),
                  pl.BlockSpec((1, LANES), lambda i, j: (0, 0))],
        out_specs=[pl.BlockSpec((None, tb, LANES), lambda i, j: (i, j, 0)),
                   pl.BlockSpec((None, tb, LANES), lambda i, j: (i, j, 0))],
        out_shape=[jax.ShapeDtypeStruct((b, t, LANES), F32),
                   jax.ShapeDtypeStruct((b, t, LANES), F32)],
        scratch_shapes=[pltpu.VMEM((1, LANES), F32)],
        compiler_params=_cparams(("parallel", "arbitrary")),
        name="logf_cumsum",
    )(z3, bf)


def _fox_prompt_kernel(q_ref, k_ref, vt_ref, ck_ref, o_ref, m_ref, l_ref, acc_ref, *, tq, group, first_key):
    qi = pl.program_id(1)
    ki = pl.program_id(2)

    @pl.when(ki == 0)
    def _():
        m_ref[...] = jnp.full_like(m_ref, NEG)
        l_ref[...] = jnp.zeros_like(l_ref)
        acc_ref[...] = jnp.zeros_like(acc_ref)

    @pl.when(ki <= qi)
    def _():
        ck = ck_ref[...] * LOG2E
        kpos = ki * tq + lax.broadcasted_iota(I32, (tq, tq), 0)
        qpos = qi * tq + lax.broadcasted_iota(I32, (tq, tq), 1)
        mask = (qpos >= kpos) & (kpos >= first_key)
        for g in range(N_HEADS // group):
            heads = range(g * group, (g + 1) * group)
            sl = {h: slice(h * HEAD_DIM, (h + 1) * HEAD_DIM) for h in heads}
            m_old = {h: m_ref[h] for h in heads}
            l_old = {h: l_ref[h] for h in heads}
            acc_old = {h: acc_ref[h] for h in heads}
            s = {h: _bdot_nt(k_ref[:, sl[h]], q_ref[:, sl[h]]) * (HEAD_DIM ** -0.5 * LOG2E) - ck[:, h:h + 1]
                 for h in heads}
            s = {h: jnp.where(mask, s[h], NEG) for h in heads}
            m_new = {h: jnp.maximum(m_old[h], jnp.max(s[h], axis=0, keepdims=True)) for h in heads}
            alpha = {h: jnp.exp2(m_old[h] - m_new[h]) for h in heads}
            p = {h: jnp.exp2(s[h] - m_new[h]) for h in heads}
            l_new = {h: alpha[h] * l_old[h] + jnp.sum(p[h], axis=0, keepdims=True) for h in heads}
            acc_new = {h: alpha[h] * acc_old[h] + _bdot(vt_ref[h], p[h]) for h in heads}
            for h in heads:
                m_ref[h] = m_new[h]
                l_ref[h] = l_new[h]
                acc_ref[h] = acc_new[h]

    @pl.when(ki == qi)
    def _():
        for h in range(N_HEADS):
            o_ref[:, h * HEAD_DIM:(h + 1) * HEAD_DIM] = (acc_ref[h] / l_ref[h]).T


def fox_prompt(z3, cum, tq, first_key, group=4):
    b, t, _ = z3.shape
    hd = N_HEADS * HEAD_DIM
    v_t = jnp.transpose(z3[..., 2 * hd:3 * hd].reshape(b, t, N_HEADS, HEAD_DIM), (0, 2, 3, 1))
    nq = _exact_div(t, tq)
    kern = functools.partial(_fox_prompt_kernel, tq=tq, group=group, first_key=first_key)
    return pl.pallas_call(
        kern,
        grid=(b, nq, nq),
        in_specs=[pl.BlockSpec((None, tq, hd), lambda i, qi, ki: (i, qi, 0)),
                  pl.BlockSpec((None, tq, hd), lambda i, qi, ki: (i, jnp.minimum(ki, qi), 1)),
                  pl.BlockSpec((None, N_HEADS, HEAD_DIM, tq), lambda i, qi, ki: (i, 0, 0, jnp.minimum(ki, qi))),
                  pl.BlockSpec((None, tq, LANES), lambda i, qi, ki: (i, jnp.minimum(ki, qi), 0))],
        out_specs=pl.BlockSpec((None, tq, hd), lambda i, qi, ki: (i, qi, 0)),
        out_shape=jax.ShapeDtypeStruct((b, t, hd), F32),
        scratch_shapes=[pltpu.VMEM((N_HEADS, 1, tq), F32),
                        pltpu.VMEM((N_HEADS, 1, tq), F32),
                        pltpu.VMEM((N_HEADS, HEAD_DIM, tq), F32)],
        compiler_params=_cparams(("parallel", "parallel", "arbitrary")),
        name="fox_prompt",
    )(z3, z3, v_t, cum)


def _strided_lane_cumsum(x, n_pos):
    lane = lax.broadcasted_iota(I32, x.shape, 1)
    sh = N_HEADS
    while sh < n_pos * N_HEADS:
        x = x + jnp.where(lane >= sh, pltpu.roll(x, sh, axis=1), 0.0)
        sh *= 2
    return x


def _page_cumsum(lf, carry):
    x = _strided_lane_cumsum(lf, LANES // N_HEADS)
    lane = lax.broadcasted_iota(I32, lf.shape, 1)
    row = lax.broadcasted_iota(I32, lf.shape, 0)
    tot = jnp.where(lane >= LANES - N_HEADS, x, 0.0)
    sh = N_HEADS
    while sh < LANES:
        tot = tot + pltpu.roll(tot, LANES - sh, axis=1)
        sh *= 2
    incl = tot
    sh = 1
    while sh < SUBLANES:
        incl = incl + jnp.where(row >= sh, pltpu.roll(incl, sh, axis=0), 0.0)
        sh *= 2
    cum = x + (incl - tot) + carry
    new_carry = carry + jnp.broadcast_to(incl[SUBLANES - 1:SUBLANES, :], carry.shape)
    return cum, new_carry


def _fox_sample_kernel(pt_ref, q_ref, kn_ref, vn_ref, lfn_ref, *rest, sn, pps):
    del pt_ref
    kp_refs, vp_refs, lfp_refs = rest[:pps], rest[pps:2 * pps], rest[2 * pps:3 * pps]
    o_ref, m_ref, l_ref, acc_ref, carry_ref = rest[3 * pps:]
    p = pl.program_id(1)
    rows = sn * N_HEADS
    keys = PAGE * N_HEADS
    scale = HEAD_DIM ** -0.5

    @pl.when(p == 0)
    def _():
        m_ref[...] = jnp.full_like(m_ref, NEG)
        l_ref[...] = jnp.zeros_like(l_ref)
        acc_ref[...] = jnp.zeros_like(acc_ref)
        carry_ref[...] = jnp.zeros_like(carry_ref)

    row_head = lax.broadcasted_iota(I32, (rows, LANES), 0) % N_HEADS
    lane_head = lax.broadcasted_iota(I32, (rows, LANES), 1) % N_HEADS
    same_head = row_head == lane_head

    def update(scores, vals):
        m_old = m_ref[...]
        m_new = m_old
        for sc in scores:
            m_new = jnp.maximum(m_new, jnp.max(sc, axis=-1, keepdims=True))
        alpha = jnp.exp(m_old - m_new)
        l_new = alpha * l_ref[...]
        acc = alpha * acc_ref[...]
        for sc, vl in zip(scores, vals):
            pr = jnp.exp(sc - m_new)
            l_new = l_new + jnp.sum(pr, axis=-1, keepdims=True)
            acc = acc + _dot(pr.astype(BF16), vl)
        l_ref[...] = l_new
        acc_ref[...] = acc
        m_ref[...] = m_new

    q = q_ref[...].reshape(rows, HEAD_DIM).astype(BF16)
    carry = carry_ref[...]
    scores, vals = [], []
    for r in range(pps):
        cum, carry = _page_cumsum(lfp_refs[r][...], carry)
        k2 = kp_refs[r][...].reshape(keys, HEAD_DIM).astype(BF16)
        sc = _dot_nt(q, k2) * scale
        tiles = []
        for t in range(keys // LANES):
            tile = sc[:, t * LANES:(t + 1) * LANES] - cum[t:t + 1, :]
            tiles.append(jnp.where(same_head, tile, NEG))
        scores.append(jnp.concatenate(tiles, axis=1))
        vals.append(vp_refs[r][...].reshape(keys, HEAD_DIM).astype(BF16))
    update(scores, vals)
    carry_ref[...] = carry

    @pl.when(p == pl.num_programs(1) - 1)
    def _():
        pad = jnp.zeros((LANES - rows, HEAD_DIM), F32)
        kn = jnp.concatenate([kn_ref[...].reshape(rows, HEAD_DIM), pad], axis=0).astype(BF16)
        vn = jnp.concatenate([vn_ref[...].reshape(rows, HEAD_DIM), pad], axis=0).astype(BF16)
        cnew = _strided_lane_cumsum(lfn_ref[...], sn) + carry_ref[0:1, :]
        scn = _dot_nt(q, kn) * scale - cnew
        q_pos = lax.broadcasted_iota(I32, (rows, LANES), 0) // N_HEADS
        k_pos = lax.broadcasted_iota(I32, (rows, LANES), 1) // N_HEADS
        update([jnp.where(same_head & (k_pos <= q_pos), scn, NEG)], [vn])
        o_ref[...] = (acc_ref[...] / l_ref[...]).reshape(sn, N_HEADS, HEAD_DIM)


def fox_sample(qkv_s, lfn, k_pool, v_pool, lf_pool, page_table, pps=4):
    b, sn = qkv_s.shape[:2]
    n_pages = page_table.shape[1]
    rows = sn * N_HEADS
    kern = functools.partial(_fox_sample_kernel, sn=sn, pps=pps)
    page_map = lambda r: (lambda i, p, pt: (pt[i, p * pps + r], 0, 0, 0))
    lf_map = lambda r: (lambda i, p, pt: (pt[i, p * pps + r], 0, 0))
    new_spec = lambda part: pl.BlockSpec((None, sn, N_HEADS, HEAD_DIM), lambda i, p, pt: (i, 0, part, 0))
    grid_spec = pltpu.PrefetchScalarGridSpec(
        num_scalar_prefetch=1,
        grid=(b, _exact_div(n_pages, pps)),
        in_specs=[new_spec(0), new_spec(1), new_spec(2),
                  pl.BlockSpec((None, 1, LANES), lambda i, p, pt: (i, 0, 0))]
                 + [pl.BlockSpec((None, PAGE, N_HEADS, HEAD_DIM), page_map(r)) for r in range(pps)]
                 + [pl.BlockSpec((None, PAGE, N_HEADS, HEAD_DIM), page_map(r)) for r in range(pps)]
                 + [pl.BlockSpec((None, SUBLANES, LANES), lf_map(r)) for r in range(pps)],
        out_specs=pl.BlockSpec((None, sn, N_HEADS, HEAD_DIM), lambda i, p, pt: (i, 0, 0, 0)),
        scratch_shapes=[pltpu.VMEM((rows, 1), F32),
                        pltpu.VMEM((rows, 1), F32),
                        pltpu.VMEM((rows, HEAD_DIM), F32),
                        pltpu.VMEM((SUBLANES, LANES), F32)],
    )
    return pl.pallas_call(
        kern,
        grid_spec=grid_spec,
        out_shape=jax.ShapeDtypeStruct((b, sn, N_HEADS, HEAD_DIM), F32),
        compiler_params=_cparams(("parallel", "arbitrary")),
        name="fox_sample",
    )(page_table, qkv_s, qkv_s, qkv_s, lfn, *([k_pool] * pps), *([v_pool] * pps), *([lf_pool] * pps))


def _top_rows(x, payload, k):
    r = x.shape[0]
    row = lax.broadcasted_iota(I32, x.shape, 0)
    vals, picks = [], []
    for _ in range(k):
        m = jnp.max(x, axis=0, keepdims=True)
        pos = jnp.min(jnp.where(x == m, row, r), axis=0, keepdims=True)
        sel = row == pos
        vals.append(m)
        picks.append(pos if payload is None else jnp.max(jnp.where(sel, payload, -1), axis=0, keepdims=True))
        x = jnp.where(sel, -jnp.inf, x)
    return jnp.concatenate(vals, axis=0), jnp.concatenate(picks, axis=0)


def _peer_topk_kernel(q_ref, keys_ref, off_ref, gate_ref):
    offs, gates = [], []
    for h in range(N_HEADS):
        ts, ti = [], []
        for half in range(2):
            col = (2 * h + half) * N_KEYS
            sc = _dot_nt(keys_ref[half], q_ref[:, col:col + N_KEYS], HI)
            v, i = _top_rows(sc, None, PEER_TOPK)
            ts.append(v)
            ti.append(i)
        half_k = PEER_TOPK // 2
        cand_s = [ts[0][0:1] + ts[1]]
        cand_i = [ti[0][0:1] * N_KEYS + ti[1]]
        for a in range(1, half_k):
            cand_s.append(ts[0][a:a + 1] + ts[1][:half_k])
            cand_i.append(ti[0][a:a + 1] * N_KEYS + ti[1][:half_k])
        cand_s.append(ts[0][half_k:] + ts[1][0:1])
        cand_i.append(ti[0][half_k:] * N_KEYS + ti[1][0:1])
        best_s, best_i = _top_rows(jnp.concatenate(cand_s, axis=0), jnp.concatenate(cand_i, axis=0), PEER_TOPK)
        e = jnp.exp(best_s - best_s[0:1])
        gates.append(e / jnp.sum(e, axis=0, keepdims=True))
        offs.append(best_i * ROW_WORDS)
    off_ref[...] = jnp.concatenate(offs, axis=0).T
    gate_ref[...] = jnp.concatenate(gates, axis=0).T


def peer_topk(q, sub_keys, tn=128):
    n, dq = q.shape
    return pl.pallas_call(
        _peer_topk_kernel,
        grid=(_exact_div(n, tn),),
        in_specs=[pl.BlockSpec((tn, dq), lambda i: (i, 0)),
                  pl.BlockSpec((2, N_KEYS, N_KEYS), lambda i: (0, 0, 0))],
        out_specs=[pl.BlockSpec((tn, PEER_SLOTS), lambda i: (i, 0)),
                   pl.BlockSpec((tn, PEER_SLOTS), lambda i: (i, 0))],
        out_shape=[jax.ShapeDtypeStruct((n, PEER_SLOTS), I32),
                   jax.ShapeDtypeStruct((n, PEER_SLOTS), F32)],
        compiler_params=_cparams(("parallel",)),
        name="peer_topk",
    )(q, sub_keys)


def pack_expert_table(t):
    e, d = t.shape
    tb = t.astype(BF16).reshape(e, ROW_WORDS, 2, LANES)
    words = lax.bitcast_convert_type(jnp.swapaxes(tb, 2, 3), I32)
    return words.reshape(e * ROW_WORDS, LANES)


def _gather_rows(row_ref, tbl_ref, stage_ref, n):
    offs = row_ref.at[n]
    for j in range(PEER_SLOTS):
        start = pl.multiple_of(offs[j], ROW_WORDS)
        stage_ref[j * ROW_WORDS:(j + 1) * ROW_WORDS, :] = tbl_ref[pl.ds(start, ROW_WORDS), :]


def _chunk_mask(d):
    q = lax.broadcasted_iota(I32, (SUBLANES, d), 0)
    c = lax.broadcasted_iota(I32, (SUBLANES, d), 1)
    return (c % SUBLANES == q).astype(F32)


def _for_token_groups(tb, idx_ref, tbl_ref, stage_refs, consume):
    group = len(stage_refs)

    def body(i, carry):
        base = i * group
        _gather_rows(idx_ref, tbl_ref, stage_refs[0], base)
        for t in range(group):
            if t + 1 < group:
                _gather_rows(idx_ref, tbl_ref, stage_refs[t + 1], base + t + 1)
            consume(base + t, stage_refs[t])
        return carry

    lax.fori_loop(0, _exact_div(tb, group), body, 0)


def _peer_act_kernel(idx_ref, x_ref, gate_ref, tbl_ref, rep_t_ref, rep_ref, w_ref, r_ref, *stage_refs, tb):
    d = PEER_SLOTS * SUBLANES
    mask = _chunk_mask(d)

    def consume(n, stage_ref):
        rows = pltpu.bitcast(stage_ref[...], BF16)
        out = _dot_nt(x_ref[n].astype(BF16), rows)
        r_ref[n] = out * mask

    _for_token_groups(tb, idx_ref, tbl_ref, stage_refs, consume)
    part = _split_dot(r_ref[...].reshape(tb * SUBLANES, d), rep_t_ref[...])
    act = jnp.sum(part.reshape(tb, SUBLANES, PEER_SLOTS), axis=1)
    w = gate_ref[...] * _gelu_exact(act)
    w_ref[...] = _dot(w.astype(BF16), rep_ref[...])


def _peer_mix_kernel(idx_ref, w_ref, res_ref, tbl_ref, o_ref, *stage_refs, tb):
    d = PEER_SLOTS * SUBLANES
    mask = _chunk_mask(LANES)

    def consume(n, stage_ref):
        rows = pltpu.bitcast(stage_ref[...], BF16)
        wn = w_ref[n]
        wsel = jnp.concatenate(
            [jnp.broadcast_to(wn[t:t + 1, :], (SUBLANES, LANES)) * mask for t in range(d // LANES)], axis=1)
        o_ref[n] = res_ref[n] + _dot(wsel.astype(BF16), rows)

    _for_token_groups(tb, idx_ref, tbl_ref, stage_refs, consume)


def peer_experts(idx, gate, xn, res, u_tbl, v_tbl, tb=64):
    n, dm = xn.shape
    d = PEER_SLOTS * SUBLANES
    chunks = dm // LANES
    slot_of_col = jnp.arange(d, dtype=I32) // SUBLANES
    rep = (slot_of_col[None, :] == jnp.arange(PEER_SLOTS, dtype=I32)[:, None]).astype(BF16)
    tbl_spec = pl.BlockSpec(u_tbl.shape, lambda i: (0, 0), pipeline_mode=pl.Buffered(1))
    smem_idx = pl.BlockSpec((tb, PEER_SLOTS), lambda i: (i, 0), memory_space=pltpu.SMEM)
    stages = [pltpu.VMEM((PEER_SLOTS * ROW_WORDS, LANES), I32)] * PEER_TOKEN_GROUP
    w = pl.pallas_call(
        functools.partial(_peer_act_kernel, tb=tb),
        grid=(_exact_div(n, tb),),
        in_specs=[smem_idx,
                  pl.BlockSpec((tb, chunks, LANES), lambda i: (i, 0, 0)),
                  pl.BlockSpec((tb, PEER_SLOTS), lambda i: (i, 0)),
                  tbl_spec,
                  pl.BlockSpec((d, PEER_SLOTS), lambda i: (0, 0)),
                  pl.BlockSpec((PEER_SLOTS, d), lambda i: (0, 0))],
        out_specs=pl.BlockSpec((tb, d), lambda i: (i, 0)),
        out_shape=jax.ShapeDtypeStruct((n, d), F32),
        scratch_shapes=[pltpu.VMEM((tb, SUBLANES, d), F32)] + stages,
        compiler_params=_cparams(("arbitrary",)),
        name="peer_act",
    )(idx, xn.reshape(n, chunks, LANES), gate, u_tbl, rep.T, rep)
    out = pl.pallas_call(
        functools.partial(_peer_mix_kernel, tb=tb),
        grid=(_exact_div(n, tb),),
        in_specs=[smem_idx,
                  pl.BlockSpec((tb, d // LANES, LANES), lambda i: (i, 0, 0)),
                  pl.BlockSpec((tb, chunks, LANES), lambda i: (i, 0, 0)),
                  tbl_spec],
        out_specs=pl.BlockSpec((tb, chunks, LANES), lambda i: (i, 0, 0)),
        out_shape=jax.ShapeDtypeStruct((n, chunks, LANES), F32),
        scratch_shapes=stages,
        compiler_params=_cparams(("arbitrary",)),
        name="peer_mix",
    )(idx, w.reshape(n, d // LANES, LANES), res.reshape(n, chunks, LANES), v_tbl)
    return out.reshape(n, dm)


def peer_ffn(hs, norm_g, w_q, sub_keys, u, v):
    w_q = w_q.astype(BF16)
    u_tbl, v_tbl = pack_expert_table(u), pack_expert_table(v)
    outs = []
    for h in hs:
        q, xn = norm_matmul(h, norm_g, w_q, with_xn=True)
        offs, gate = peer_topk(q, sub_keys)
        outs.append(peer_experts(offs, gate, xn, h, u_tbl, v_tbl))
    return outs


def kernel(x_prompt, x_sample, state_conv, state_ssm, cache_k, cache_v, cache_logf, page_table, meta_tokens,
           norm_mix, norm_ffn, norm_final, gdn_w_in, gdn_conv_w, gdn_a_log, gdn_dt_bias, gdn_out_norm,
           gdn_w_out, fox_w_in, fox_b_f, fox_w_out, peer_w_q, peer_sub_keys, peer_u, peer_v):
    bp, seq, dm = x_prompt.shape
    bs, sn, _ = x_sample.shape
    hd = N_HEADS * HEAD_DIM
    tp = LEAD_PAD + N_META + seq
    first = LEAD_PAD
    n_p = bp * tp
    n_s = bs * sn

    meta = jnp.broadcast_to(meta_tokens[None], (bp, N_META, dm))
    h_p = jnp.concatenate([jnp.zeros((bp, LEAD_PAD, dm), F32), meta, x_prompt], axis=1).reshape(n_p, dm)
    h_s = x_sample.reshape(n_s, dm)

    w_in = gdn_w_in[0]
    wg = jnp.pad(w_in, ((0, 0), (0, 4 * hd + LANES - w_in.shape[1]))).astype(BF16)
    w_out = gdn_w_out[0].astype(BF16)
    z_p2 = norm_matmul(h_p, norm_mix[0], wg)
    z_s2 = norm_matmul(h_s, norm_mix[0], wg)
    zw = z_p2.shape[1]
    z_p = z_p2.reshape(bp, tp, zw)
    z_s = jnp.pad(z_s2.reshape(bs, sn, zw), ((0, 0), (0, SAMPLE_CHUNK - sn), (0, 0)))
    qkv_p, gb_p = gdn_prep(z_p, z_p, gdn_conv_w[0], gdn_a_log[0], gdn_dt_bias[0],
                           tb=128, zero_first=True, t_lo=first, t_hi=tp)
    o_p, ssm_p = gdn_chunks(qkv_p, gb_p, jnp.zeros((bp, N_HEADS, HEAD_DIM, HEAD_DIM), F32), GDN_CHUNK)
    prev_s = jnp.pad(state_conv[0], ((0, 0), (SUBLANES - (CONV_W - 1), 0), (0, 0)))
    qkv_s, gb_s = gdn_prep(z_s, prev_s, gdn_conv_w[0], gdn_a_log[0], gdn_dt_bias[0],
                           tb=SAMPLE_CHUNK, zero_first=False, t_lo=0, t_hi=sn)
    o_s, ssm_s = gdn_chunks(qkv_s, gb_s, state_ssm[0], SAMPLE_CHUNK)
    h_p = gdn_out(o_p.reshape(n_p, hd), z_p2, 3, gdn_out_norm[0], w_out, h_p)
    h_s = gdn_out(o_s[:, :sn].reshape(n_s, hd), z_s2, 3, gdn_out_norm[0], w_out, h_s)
    new_conv_p = z_p[:, tp - (CONV_W - 1):, :3 * hd][None]
    new_conv_s = z_s[:, sn - (CONV_W - 1):sn, :3 * hd][None]
    h_p, h_s = peer_ffn([h_p, h_s], norm_ffn[0], peer_w_q[0], peer_sub_keys[0], peer_u[0], peer_v[0])

    w_in = fox_w_in[0]
    wf = jnp.pad(w_in, ((0, 0), (0, 3 * hd + LANES - w_in.shape[1]))).astype(BF16)
    w_out = fox_w_out[0].astype(BF16)
    z_p2 = norm_matmul(h_p, norm_mix[1], wf)
    zw = z_p2.shape[1]
    z_p = z_p2.reshape(bp, tp, zw)
    z_s = norm_matmul(h_s, norm_mix[1], wf).reshape(bs, sn, zw)
    lf_p, cum_p = logf_cumsum(z_p, fox_b_f[0], (3 * hd) // LANES, tb=128)
    o_p = fox_prompt(z_p, cum_p, tq=384, first_key=first)
    z_s8 = jnp.pad(z_s, ((0, 0), (0, SUBLANES - sn), (0, 0)))
    lf_s, _ = logf_cumsum(z_s8, fox_b_f[0], (3 * hd) // LANES, tb=SUBLANES)
    lf_s = lf_s[:, :sn, :N_HEADS]
    lfn = jnp.pad(lf_s.reshape(bs, 1, sn * N_HEADS), ((0, 0), (0, 0), (0, LANES - sn * N_HEADS)))
    o_s = fox_sample(z_s[..., :3 * hd].reshape(bs, sn, 3 * N_HEADS, HEAD_DIM), lfn,
                     cache_k.reshape(-1, PAGE, N_HEADS, HEAD_DIM), cache_v.reshape(-1, PAGE, N_HEADS, HEAD_DIM),
                     cache_logf.reshape(-1, SUBLANES, LANES), page_table)
    h_p = matmul_residual(o_p.reshape(n_p, hd), w_out, h_p)
    h_s = matmul_residual(o_s.reshape(n_s, hd), w_out, h_s)
    kv_p = z_p[:, first:, hd:3 * hd]
    new_k_p = kv_p[..., :hd].reshape(1, bp, tp - first, N_HEADS, HEAD_DIM)
    new_v_p = kv_p[..., hd:].reshape(1, bp, tp - first, N_HEADS, HEAD_DIM)
    new_logf_p = lf_p[:, first:, :N_HEADS][None]
    new_k_s = z_s[..., hd:2 * hd].reshape(1, bs, sn, N_HEADS, HEAD_DIM)
    new_v_s = z_s[..., 2 * hd:3 * hd].reshape(1, bs, sn, N_HEADS, HEAD_DIM)
    new_logf_s = lf_s[None]
    h_p, h_s = peer_ffn([h_p, h_s], norm_ffn[1], peer_w_q[1], peer_sub_keys[1], peer_u[1], peer_v[1])

    y_prompt = final_norm(h_p.reshape(bp, tp, dm), norm_final, (LEAD_PAD + N_META) // 128, seq)
    y_sample = final_norm(h_s.reshape(1, n_s, dm), norm_final, 0, n_s).reshape(bs, sn, dm)
    return (y_prompt, y_sample, new_conv_p, ssm_p[None], new_k_p, new_v_p, new_logf_p,
            new_conv_s, ssm_s[None], new_k_s, new_v_s, new_logf_s)
```

```python
import functools
import math

import jax
import jax.numpy as jnp
from jax import lax
from jax.experimental import pallas as pl
from jax.experimental.pallas import tpu as pltpu

F32 = jnp.float32
BF16 = jnp.bfloat16
I32 = jnp.int32
HI = lax.Precision.HIGHEST

EPS = 1e-6
LANES = 128
SUBLANES = 8
VMEM_LIMIT = 56 * 1024 * 1024

N_META = 16
LEAD_PAD = 112
HEAD_DIM = 128
N_HEADS = 8
CONV_W = 4
GDN_CHUNK = 64
SAMPLE_CHUNK = 16
INV_BLOCK = 16
PAGE = 128
PEER_TOPK = 16
N_KEYS = 128
PEER_SLOTS = N_HEADS * PEER_TOPK
ROW_WORDS = 4
PEER_TOKEN_GROUP = 16
ROUTE_PAIR = 2
NEG = -1e30
LOG2E = 1.4426950408889634


def _exact_div(n, d):
    assert n % d == 0, (n, d)
    return n // d


def _cparams(sem):
    return pltpu.CompilerParams(dimension_semantics=sem, vmem_limit_bytes=VMEM_LIMIT)


def _dot(a, b, precision=None):
    return jnp.dot(a, b, preferred_element_type=F32, precision=precision)


def _dot_nt(a, b, precision=None):
    return lax.dot_general(a, b, (((1,), (1,)), ((), ())), preferred_element_type=F32, precision=precision)


def _dot_tn(a, b, precision=None):
    return lax.dot_general(a, b, (((0,), (0,)), ((), ())), preferred_element_type=F32, precision=precision)


def _split_dot(a, b01):
    hi = a.astype(BF16)
    lo = (a - hi.astype(F32)).astype(BF16)
    return _dot(hi, b01) + _dot(lo, b01)


def _dot3(a, b):
    a_hi = a.astype(BF16)
    b_hi = b.astype(BF16)
    a_lo = (a - a_hi.astype(F32)).astype(BF16)
    b_lo = (b - b_hi.astype(F32)).astype(BF16)
    return _dot(a_hi, b_hi) + (_dot(a_hi, b_lo) + _dot(a_lo, b_hi))


def _softplus(x):
    return jnp.maximum(x, 0.0) + jnp.log1p(jnp.exp(-jnp.abs(x)))


def _log_sigmoid(x):
    return -_softplus(-x)


def _gelu_exact(x):
    return 0.5 * x * (1.0 + lax.erf(x * (2.0 ** -0.5)))


def _norm_matmul_kernel(x_ref, g_ref, w_ref, o_ref, *xn_ref, slabs):
    x = x_ref[...]
    xn = x * lax.rsqrt(jnp.mean(x * x, axis=-1, keepdims=True) + EPS) * g_ref[...]
    if xn_ref:
        xn_ref[0][...] = xn
    y = _dot(xn.astype(BF16), w_ref[...])
    if slabs:
        for s in range(y.shape[1] // LANES):
            o_ref[s] = y[:, s * LANES:(s + 1) * LANES]
    else:
        o_ref[...] = y


def norm_matmul(x, g, w_bf16, tm=256, with_xn=False, slabs=False):
    n, k = x.shape
    m = w_bf16.shape[1]
    if slabs:
        out_shape = [jax.ShapeDtypeStruct((m // LANES, n, LANES), F32)]
        out_specs = [pl.BlockSpec((m // LANES, tm, LANES), lambda i: (0, i, 0))]
    else:
        out_shape = [jax.ShapeDtypeStruct((n, m), F32)]
        out_specs = [pl.BlockSpec((tm, m), lambda i: (i, 0))]
    if with_xn:
        out_shape.append(jax.ShapeDtypeStruct((n, k), F32))
        out_specs.append(pl.BlockSpec((tm, k), lambda i: (i, 0)))
    res = pl.pallas_call(
        functools.partial(_norm_matmul_kernel, slabs=slabs),
        grid=(_exact_div(n, tm),),
        in_specs=[pl.BlockSpec((tm, k), lambda i: (i, 0)),
                  pl.BlockSpec((1, k), lambda i: (0, 0)),
                  pl.BlockSpec((k, m), lambda i: (0, 0))],
        out_specs=out_specs,
        out_shape=out_shape,
        compiler_params=_cparams(("parallel",)),
        name="norm_matmul",
    )(x, g.reshape(1, k), w_bf16)
    return res if with_xn else res[0]


def _matmul_res_kernel(x_ref, w_ref, r_ref, o_ref):
    o_ref[...] = r_ref[...] + _dot(x_ref[...].astype(BF16), w_ref[...])


def matmul_residual(x, w_bf16, res, tm=256):
    n, k = x.shape
    m = w_bf16.shape[1]
    return pl.pallas_call(
        _matmul_res_kernel,
        grid=(_exact_div(n, tm),),
        in_specs=[pl.BlockSpec((tm, k), lambda i: (i, 0)),
                  pl.BlockSpec((k, m), lambda i: (0, 0)),
                  pl.BlockSpec((tm, m), lambda i: (i, 0))],
        out_specs=pl.BlockSpec((tm, m), lambda i: (i, 0)),
        out_shape=jax.ShapeDtypeStruct((n, m), F32),
        compiler_params=_cparams(("parallel",)),
        name="matmul_residual",
    )(x, w_bf16, res)


def _gdn_out_kernel(o_ref, gate_ref, nw_ref, w_ref, r_ref, y_ref):
    o = o_ref[...]
    gate = gate_ref[...]
    nw = nw_ref[...]
    parts = []
    for h in range(N_HEADS):
        sl = slice(h * HEAD_DIM, (h + 1) * HEAD_DIM)
        oh = o[:, sl]
        oh = oh * lax.rsqrt(jnp.mean(oh * oh, axis=-1, keepdims=True) + EPS) * nw
        parts.append(oh * jax.nn.silu(gate[:, sl]))
    x = jnp.concatenate(parts, axis=-1)
    y_ref[...] = r_ref[...] + _dot(x.astype(BF16), w_ref[...])


def gdn_out(o, z, gate_col_block, out_norm, w_bf16, res, tm=256):
    n, d = o.shape
    return pl.pallas_call(
        _gdn_out_kernel,
        grid=(_exact_div(n, tm),),
        in_specs=[pl.BlockSpec((tm, d), lambda i: (i, 0)),
                  pl.BlockSpec((tm, d), lambda i: (i, gate_col_block)),
                  pl.BlockSpec((1, HEAD_DIM), lambda i: (0, 0)),
                  pl.BlockSpec((d, d), lambda i: (0, 0)),
                  pl.BlockSpec((tm, d), lambda i: (i, 0))],
        out_specs=pl.BlockSpec((tm, d), lambda i: (i, 0)),
        out_shape=jax.ShapeDtypeStruct((n, d), F32),
        compiler_params=_cparams(("parallel",)),
        name="gdn_out",
    )(o, z, out_norm.reshape(1, HEAD_DIM), w_bf16, res)


def _final_norm_kernel(x_ref, g_ref, o_ref):
    x = x_ref[...]
    o_ref[...] = x * lax.rsqrt(jnp.mean(x * x, axis=-1, keepdims=True) + EPS) * g_ref[...]


def final_norm(x3, g, row_block_offset, t_out, tb=128):
    b, _, d = x3.shape
    return pl.pallas_call(
        _final_norm_kernel,
        grid=(b, t_out // tb),
        in_specs=[pl.BlockSpec((None, tb, d), lambda i, j: (i, j + row_block_offset, 0)),
                  pl.BlockSpec((1, d), lambda i, j: (0, 0))],
        out_specs=pl.BlockSpec((None, tb, d), lambda i, j: (i, j, 0)),
        out_shape=jax.ShapeDtypeStruct((b, t_out, d), F32),
        compiler_params=_cparams(("parallel", "parallel")),
        name="final_norm",
    )(x3, g.reshape(1, d))


def _gdn_prep_kernel(prev_ref, cur_ref, ab_ref, convw_ref, par_ref, qkv_ref, gb_ref, *,
                     tb, zero_first, t_lo, t_hi):
    t = pl.program_id(1)
    cur = cur_ref[...]
    prev = prev_ref[...]
    if zero_first:
        prev = jnp.where(t == 0, 0.0, prev)
    cat = jnp.concatenate([prev, cur], axis=0)
    w = convw_ref[...]
    y = None
    for i in range(CONV_W):
        shift = CONV_W - 1 - i
        xi = cur if shift == 0 else pltpu.roll(cat, shift, axis=0)[SUBLANES:]
        term = xi * w[i:i + 1]
        y = term if y is None else y + term
    y = jax.nn.silu(y)
    hd = N_HEADS * HEAD_DIM
    for h in range(2 * N_HEADS):
        sl = slice(h * HEAD_DIM, (h + 1) * HEAD_DIM)
        xh = y[:, sl]
        xh = xh * lax.rsqrt(jnp.sum(xh * xh, axis=-1, keepdims=True) + EPS)
        if h < N_HEADS:
            xh = xh * (HEAD_DIM ** -0.5)
        qkv_ref[:, sl] = xh
    qkv_ref[:, 2 * hd:] = y[:, 2 * hd:]
    ab = ab_ref[...]
    a_log = par_ref[0:1, :]
    dt_bias = par_ref[1:2, :]
    g = -jnp.exp(a_log) * _softplus(ab + dt_bias)
    beta = jax.nn.sigmoid(ab)
    lane = lax.broadcasted_iota(I32, ab.shape, 1)
    pos = t * tb + lax.broadcasted_iota(I32, ab.shape, 0)
    valid = (pos >= t_lo) & (pos < t_hi)
    gb = jnp.where(lane < N_HEADS, g, jnp.where(lane < 2 * N_HEADS, beta, 0.0))
    gb_ref[...] = jnp.where(valid, gb, 0.0)


def gdn_prep(z3, prev_src, conv_w, a_log, dt_bias, tb, zero_first, t_lo, t_hi):
    b, t, _ = z3.shape
    c = 3 * N_HEADS * HEAD_DIM
    ab_block = (4 * N_HEADS * HEAD_DIM) // LANES
    par = jnp.zeros((SUBLANES, LANES), F32)
    par = par.at[0, :N_HEADS].set(a_log).at[1, :N_HEADS].set(dt_bias)
    if zero_first:
        prev_map = lambda i, j: (i, jnp.maximum(j * (tb // SUBLANES) - 1, 0), 0)
    else:
        prev_map = lambda i, j: (i, 0, 0)
    kern = functools.partial(_gdn_prep_kernel, tb=tb, zero_first=zero_first, t_lo=t_lo, t_hi=t_hi)
    return pl.pallas_call(
        kern,
        grid=(b, t // tb),
        in_specs=[pl.BlockSpec((None, SUBLANES, c), prev_map),
                  pl.BlockSpec((None, tb, c), lambda i, j: (i, j, 0)),
                  pl.BlockSpec((None, tb, LANES), lambda i, j: (i, j, ab_block)),
                  pl.BlockSpec((CONV_W, c), lambda i, j: (0, 0)),
                  pl.BlockSpec((SUBLANES, LANES), lambda i, j: (0, 0))],
        out_specs=[pl.BlockSpec((None, tb, c), lambda i, j: (i, j, 0)),
                   pl.BlockSpec((None, tb, LANES), lambda i, j: (i, j, 0))],
        out_shape=[jax.ShapeDtypeStruct((b, t, c), F32),
                   jax.ShapeDtypeStruct((b, t, LANES), F32)],
        compiler_params=_cparams(("parallel", "arbitrary")),
        name="gdn_prep",
    )(prev_src, z3, z3, conv_w, par)


def _unit_lower_inverses(mats, n):
    ii = lax.broadcasted_iota(I32, (n, n), 0)
    jj = lax.broadcasted_iota(I32, (n, n), 1)
    eye = (ii == jj).astype(F32)
    blk = min(INV_BLOCK, n)
    if n > blk:
        same = (ii // blk) == (jj // blk)
        d = [jnp.where(same, a, 0.0) for a in mats]
        rest = [jnp.where(same, 0.0, a) for a in mats]
    else:
        d = mats
    x = [eye - di for di in d]
    p = d
    width = 1
    while 2 * width < blk:
        p = [_dot3(pi, pi) for pi in p]
        x = [xi + _dot3(xi, pi) for xi, pi in zip(x, p)]
        width *= 2
    if n == blk:
        return x
    nm = [_dot3(xi, ri) for xi, ri in zip(x, rest)]
    y = [eye - ni for ni in nm]
    p = nm
    width = 1
    while 2 * width < n // blk:
        p = [_dot3(pi, pi) for pi in p]
        y = [yi + _dot3(yi, pi) for yi, pi in zip(y, p)]
        width *= 2
    return [_dot3(yi, xi) for yi, xi in zip(y, x)]


def _bdot(a, b):
    return _dot(a.astype(BF16), b.astype(BF16))


def _bdot_nt(a, b):
    return _dot_nt(a.astype(BF16), b.astype(BF16))


def _gdn_chunk_kernel(qkv_ref, gb_ref, gbt_ref, s0_ref, o_ref, sfin_ref, state_ref, *, ln):
    c = pl.program_id(1)

    @pl.when(c == 0)
    def _():
        state_ref[...] = s0_ref[...]

    hd = N_HEADS * HEAD_DIM
    heads = range(N_HEADS)
    ii = lax.broadcasted_iota(I32, (ln, ln), 0)
    jj = lax.broadcasted_iota(I32, (ln, ln), 1)
    tri_l = (ii >= jj).astype(F32)
    gb = gb_ref[...]
    gbt = gbt_ref[...]
    gc_cols = _dot(tri_l, gb, HI)
    gc_rows = _dot_nt(gbt, tri_l, HI)
    causal = ii >= jj
    strict = ii > jj
    q = [qkv_ref[:, h * HEAD_DIM:(h + 1) * HEAD_DIM] for h in heads]
    k = [qkv_ref[:, hd + h * HEAD_DIM:hd + (h + 1) * HEAD_DIM] for h in heads]
    v = [qkv_ref[:, 2 * hd + h * HEAD_DIM:2 * hd + (h + 1) * HEAD_DIM] for h in heads]
    s = [state_ref[h] for h in heads]
    beta = [gb[:, N_HEADS + h:N_HEADS + h + 1] for h in heads]
    gc = [gc_cols[:, h:h + 1] for h in heads]
    decay = [jnp.where(causal, jnp.exp(jnp.where(causal, gc[h] - gc_rows[h:h + 1, :], 0.0)), 0.0) for h in heads]
    kb = [k[h] * beta[h] for h in heads]
    a = [jnp.where(strict, _bdot_nt(kb[h], k[h]) * decay[h], 0.0) for h in heads]
    tinv = _unit_lower_inverses(a, ln)
    egc = [jnp.exp(gc[h]) for h in heads]
    u = [_dot3(tinv[h], v[h] * beta[h]) for h in heads]
    w = [_dot3(tinv[h], kb[h] * egc[h]) for h in heads]
    v_new = [u[h] - _bdot(w[h], s[h]) for h in heads]
    attn = [_bdot_nt(q[h], k[h]) * decay[h] for h in heads]
    o = [_bdot(q[h] * egc[h], s[h]) + _bdot(attn[h], v_new[h]) for h in heads]
    g_last = [gc[h][ln - 1:ln, :] for h in heads]
    k_dec = [(k[h] * jnp.exp(g_last[h] - gc[h])).T for h in heads]
    s_new = [s[h] * jnp.exp(g_last[h]) + _bdot(k_dec[h], v_new[h]) for h in heads]
    o_ref[...] = jnp.concatenate(o, axis=-1)
    for h in heads:
        state_ref[h] = s_new[h]

    @pl.when(c == pl.num_programs(1) - 1)
    def _():
        sfin_ref[...] = state_ref[...]


def gdn_chunks(qkv_c, gb, s0, ln):
    b, t, c = qkv_c.shape
    nc = t // ln
    gbt = jnp.swapaxes(gb[..., :2 * N_HEADS].reshape(b, nc, ln, 2 * N_HEADS), 2, 3)
    hd = N_HEADS * HEAD_DIM
    kern = functools.partial(_gdn_chunk_kernel, ln=ln)
    return pl.pallas_call(
        kern,
        grid=(b, nc),
        in_specs=[pl.BlockSpec((None, ln, c), lambda i, j: (i, j, 0)),
                  pl.BlockSpec((None, ln, LANES), lambda i, j: (i, j, 0)),
                  pl.BlockSpec((None, None, 2 * N_HEADS, ln), lambda i, j: (i, j, 0, 0)),
                  pl.BlockSpec((None, N_HEADS, HEAD_DIM, HEAD_DIM), lambda i, j: (i, 0, 0, 0))],
        out_specs=[pl.BlockSpec((None, ln, hd), lambda i, j: (i, j, 0)),
                   pl.BlockSpec((None, N_HEADS, HEAD_DIM, HEAD_DIM), lambda i, j: (i, 0, 0, 0))],
        out_shape=[jax.ShapeDtypeStruct((b, t, hd), F32),
                   jax.ShapeDtypeStruct((b, N_HEADS, HEAD_DIM, HEAD_DIM), F32)],
        scratch_shapes=[pltpu.VMEM((N_HEADS, HEAD_DIM, HEAD_DIM), F32)],
        compiler_params=_cparams(("parallel", "arbitrary")),
        name="gdn_chunks",
    )(qkv_c, gb, gbt, s0)


def _logf_cumsum_kernel(z_ref, bf_ref, lf_ref, cum_ref, carry_ref, *, tb):
    @pl.when(pl.program_id(1) == 0)
    def _():
        carry_ref[...] = jnp.zeros_like(carry_ref)

    lf = _log_sigmoid(z_ref[...] + bf_ref[...])
    lf_ref[...] = lf
    ii = lax.broadcasted_iota(I32, (tb, tb), 0)
    jj = lax.broadcasted_iota(I32, (tb, tb), 1)
    cum = _dot((ii >= jj).astype(F32), lf, HI) + carry_ref[...]
    cum_ref[...] = cum
    carry_ref[...] = cum[tb - 1:tb, :]


def logf_cumsum(z3, b_f, col_block, tb):
    b, t, _ = z3.shape
    bf = jnp.zeros((1, LANES), F32).at[0, :N_HEADS].set(b_f)
    kern = functools.partial(_logf_cumsum_kernel, tb=tb)
    return pl.pallas_call(
        kern,
        grid=(b, t // tb),
        in_specs=[pl.BlockSpec((None, tb, LANES), lambda i, j: (i, j, col_block)),
                  pl.BlockSpec((1, LANES), lambda i, j: (0, 0))],
        out_specs=[pl.BlockSpec((None, tb, LANES), lambda i, j: (i, j, 0)),
                   pl.BlockSpec((None, tb, LANES), lambda i, j: (i, j, 0))],
        out_shape=[jax.ShapeDtypeStruct((b, t, LANES), F32),
                   jax.ShapeDtypeStruct((b, t, LANES), F32)],
        scratch_shapes=[pltpu.VMEM((1, LANES), F32)],
        compiler_params=_cparams(("parallel", "arbitrary")),
        name="logf_cumsum",
    )(z3, bf)


def _fox_prompt_kernel(q_ref, k_ref, vt_ref, ck_ref, o_ref, m_ref, l_ref, acc_ref, *, tq, group, first_key):
    qi = pl.program_id(1)
    ki = pl.program_id(2)

    @pl.when(ki == 0)
    def _():
        m_ref[...] = jnp.full_like(m_ref, NEG)
        l_ref[...] = jnp.zeros_like(l_ref)
        acc_ref[...] = jnp.zeros_like(acc_ref)

    @pl.when(ki <= qi)
    def _():
        ck = ck_ref[...] * LOG2E
        kpos = ki * tq + lax.broadcasted_iota(I32, (tq, tq), 0)
        qpos = qi * tq + lax.broadcasted_iota(I32, (tq, tq), 1)
        mask = (qpos >= kpos) & (kpos >= first_key)
        for g in range(N_HEADS // group):
            heads = range(g * group, (g + 1) * group)
            sl = {h: slice(h * HEAD_DIM, (h + 1) * HEAD_DIM) for h in heads}
            m_old = {h: m_ref[h] for h in heads}
            l_old = {h: l_ref[h] for h in heads}
            acc_old = {h: acc_ref[h] for h in heads}
            s = {h: _bdot_nt(k_ref[:, sl[h]], q_ref[:, sl[h]]) * (HEAD_DIM ** -0.5 * LOG2E) - ck[:, h:h + 1]
                 for h in heads}
            s = {h: jnp.where(mask, s[h], NEG) for h in heads}
            m_new = {h: jnp.maximum(m_old[h], jnp.max(s[h], axis=0, keepdims=True)) for h in heads}
            alpha = {h: jnp.exp2(m_old[h] - m_new[h]) for h in heads}
            p = {h: jnp.exp2(s[h] - m_new[h]) for h in heads}
            l_new = {h: alpha[h] * l_old[h] + jnp.sum(p[h], axis=0, keepdims=True) for h in heads}
            acc_new = {h: alpha[h] * acc_old[h] + _bdot(vt_ref[h], p[h]) for h in heads}
            for h in heads:
                m_ref[h] = m_new[h]
                l_ref[h] = l_new[h]
                acc_ref[h] = acc_new[h]

    @pl.when(ki == qi)
    def _():
        for h in range(N_HEADS):
            o_ref[:, h * HEAD_DIM:(h + 1) * HEAD_DIM] = (acc_ref[h] / l_ref[h]).T


def fox_prompt(z3, cum, tq, first_key, group=4):
    b, t, _ = z3.shape
    hd = N_HEADS * HEAD_DIM
    v_t = jnp.transpose(z3[..., 2 * hd:3 * hd].reshape(b, t, N_HEADS, HEAD_DIM), (0, 2, 3, 1))
    nq = _exact_div(t, tq)
    kern = functools.partial(_fox_prompt_kernel, tq=tq, group=group, first_key=first_key)
    return pl.pallas_call(
        kern,
        grid=(b, nq, nq),
        in_specs=[pl.BlockSpec((None, tq, hd), lambda i, qi, ki: (i, qi, 0)),
                  pl.BlockSpec((None, tq, hd), lambda i, qi, ki: (i, jnp.minimum(ki, qi), 1)),
                  pl.BlockSpec((None, N_HEADS, HEAD_DIM, tq), lambda i, qi, ki: (i, 0, 0, jnp.minimum(ki, qi))),
                  pl.BlockSpec((None, tq, LANES), lambda i, qi, ki: (i, jnp.minimum(ki, qi), 0))],
        out_specs=pl.BlockSpec((None, tq, hd), lambda i, qi, ki: (i, qi, 0)),
        out_shape=jax.ShapeDtypeStruct((b, t, hd), F32),
        scratch_shapes=[pltpu.VMEM((N_HEADS, 1, tq), F32),
                        pltpu.VMEM((N_HEADS, 1, tq), F32),
                        pltpu.VMEM((N_HEADS, HEAD_DIM, tq), F32)],
        compiler_params=_cparams(("parallel", "parallel", "arbitrary")),
        name="fox_prompt",
    )(z3, z3, v_t, cum)


def _strided_lane_cumsum(x, n_pos):
    lane = lax.broadcasted_iota(I32, x.shape, 1)
    sh = N_HEADS
    while sh < n_pos * N_HEADS:
        x = x + jnp.where(lane >= sh, pltpu.roll(x, sh, axis=1), 0.0)
        sh *= 2
    return x


def _page_cumsum(lf, carry):
    x = _strided_lane_cumsum(lf, LANES // N_HEADS)
    lane = lax.broadcasted_iota(I32, lf.shape, 1)
    row = lax.broadcasted_iota(I32, lf.shape, 0)
    tot = jnp.where(lane >= LANES - N_HEADS, x, 0.0)
    sh = N_HEADS
    while sh < LANES:
        tot = tot + pltpu.roll(tot, LANES - sh, axis=1)
        sh *= 2
    incl = tot
    sh = 1
    while sh < SUBLANES:
        incl = incl + jnp.where(row >= sh, pltpu.roll(incl, sh, axis=0), 0.0)
        sh *= 2
    cum = x + (incl - tot) + carry
    new_carry = carry + jnp.broadcast_to(incl[SUBLANES - 1:SUBLANES, :], carry.shape)
    return cum, new_carry


def _fox_sample_kernel(pt_ref, q_ref, kn_ref, vn_ref, lfn_ref, *rest, sn, pps):
    del pt_ref
    kp_refs, vp_refs, lfp_refs = rest[:pps], rest[pps:2 * pps], rest[2 * pps:3 * pps]
    o_ref, m_ref, l_ref, acc_ref, carry_ref = rest[3 * pps:]
    p = pl.program_id(1)
    rows = sn * N_HEADS
    keys = PAGE * N_HEADS
    scale = HEAD_DIM ** -0.5

    @pl.when(p == 0)
    def _():
        m_ref[...] = jnp.full_like(m_ref, NEG)
        l_ref[...] = jnp.zeros_like(l_ref)
        acc_ref[...] = jnp.zeros_like(acc_ref)
        carry_ref[...] = jnp.zeros_like(carry_ref)

    row_head = lax.broadcasted_iota(I32, (rows, LANES), 0) % N_HEADS
    lane_head = lax.broadcasted_iota(I32, (rows, LANES), 1) % N_HEADS
    same_head = row_head == lane_head

    def update(scores, vals):
        m_old = m_ref[...]
        m_new = m_old
        for sc in scores:
            m_new = jnp.maximum(m_new, jnp.max(sc, axis=-1, keepdims=True))
        alpha = jnp.exp(m_old - m_new)
        l_new = alpha * l_ref[...]
        acc = alpha * acc_ref[...]
        for sc, vl in zip(scores, vals):
            pr = jnp.exp(sc - m_new)
            l_new = l_new + jnp.sum(pr, axis=-1, keepdims=True)
            acc = acc + _dot(pr.astype(BF16), vl)
        l_ref[...] = l_new
        acc_ref[...] = acc
        m_ref[...] = m_new

    q = q_ref[...].reshape(rows, HEAD_DIM).astype(BF16)
    carry = carry_ref[...]
    scores, vals = [], []
    for r in range(pps):
        cum, carry = _page_cumsum(lfp_refs[r][...], carry)
        k2 = kp_refs[r][...].reshape(keys, HEAD_DIM).astype(BF16)
        sc = _dot_nt(q, k2) * scale
        tiles = []
        for t in range(keys // LANES):
            tile = sc[:, t * LANES:(t + 1) * LANES] - cum[t:t + 1, :]
            tiles.append(jnp.where(same_head, tile, NEG))
        scores.append(jnp.concatenate(tiles, axis=1))
        vals.append(vp_refs[r][...].reshape(keys, HEAD_DIM).astype(BF16))
    update(scores, vals)
    carry_ref[...] = carry

    @pl.when(p == pl.num_programs(1) - 1)
    def _():
        pad = jnp.zeros((LANES - rows, HEAD_DIM), F32)
        kn = jnp.concatenate([kn_ref[...].reshape(rows, HEAD_DIM), pad], axis=0).astype(BF16)
        vn = jnp.concatenate([vn_ref[...].reshape(rows, HEAD_DIM), pad], axis=0).astype(BF16)
        cnew = _strided_lane_cumsum(lfn_ref[...], sn) + carry_ref[0:1, :]
        scn = _dot_nt(q, kn) * scale - cnew
        q_pos = lax.broadcasted_iota(I32, (rows, LANES), 0) // N_HEADS
        k_pos = lax.broadcasted_iota(I32, (rows, LANES), 1) // N_HEADS
        update([jnp.where(same_head & (k_pos <= q_pos), scn, NEG)], [vn])
        o_ref[...] = (acc_ref[...] / l_ref[...]).reshape(sn, N_HEADS, HEAD_DIM)


def fox_sample(qkv_s, lfn, k_pool, v_pool, lf_pool, page_table, pps=4):
    b, sn = qkv_s.shape[:2]
    n_pages = page_table.shape[1]
    rows = sn * N_HEADS
    kern = functools.partial(_fox_sample_kernel, sn=sn, pps=pps)
    page_map = lambda r: (lambda i, p, pt: (pt[i, p * pps + r], 0, 0, 0))
    lf_map = lambda r: (lambda i, p, pt: (pt[i, p * pps + r], 0, 0))
    new_spec = lambda part: pl.BlockSpec((None, sn, N_HEADS, HEAD_DIM), lambda i, p, pt: (i, 0, part, 0))
    grid_spec = pltpu.PrefetchScalarGridSpec(
        num_scalar_prefetch=1,
        grid=(b, _exact_div(n_pages, pps)),
        in_specs=[new_spec(0), new_spec(1), new_spec(2),
                  pl.BlockSpec((None, 1, LANES), lambda i, p, pt: (i, 0, 0))]
                 + [pl.BlockSpec((None, PAGE, N_HEADS, HEAD_DIM), page_map(r)) for r in range(pps)]
                 + [pl.BlockSpec((None, PAGE, N_HEADS, HEAD_DIM), page_map(r)) for r in range(pps)]
                 + [pl.BlockSpec((None, SUBLANES, LANES), lf_map(r)) for r in range(pps)],
        out_specs=pl.BlockSpec((None, sn, N_HEADS, HEAD_DIM), lambda i, p, pt: (i, 0, 0, 0)),
        scratch_shapes=[pltpu.VMEM((rows, 1), F32),
                        pltpu.VMEM((rows, 1), F32),
                        pltpu.VMEM((rows, HEAD_DIM), F32),
                        pltpu.VMEM((SUBLANES, LANES), F32)],
    )
    return pl.pallas_call(
        kern,
        grid_spec=grid_spec,
        out_shape=jax.ShapeDtypeStruct((b, sn, N_HEADS, HEAD_DIM), F32),
        compiler_params=_cparams(("parallel", "arbitrary")),
        name="fox_sample",
    )(page_table, qkv_s, qkv_s, qkv_s, lfn, *([k_pool] * pps), *([v_pool] * pps), *([lf_pool] * pps))


def _top_rows(x, payload, k):
    r = x.shape[0]
    row = lax.broadcasted_iota(I32, x.shape, 0)
    vals, picks = [], []
    for _ in range(k):
        m = jnp.max(x, axis=0, keepdims=True)
        pos = jnp.min(jnp.where(x == m, row, r), axis=0, keepdims=True)
        sel = row == pos
        vals.append(m)
        picks.append(pos if payload is None else jnp.max(jnp.where(sel, payload, -1), axis=0, keepdims=True))
        x = jnp.where(sel, -jnp.inf, x)
    return jnp.concatenate(vals, axis=0), jnp.concatenate(picks, axis=0)


def _route_scores(keys_ref, q_halves):
    return [_dot_nt(keys_ref[half], q_halves[half], HI) for half in range(2)]


def _after(x, done):
    return x if done is None else jnp.where(done == done, x, -jnp.inf)


def _route_select(scores, done=None):
    ts, ti = [], []
    for half in range(2):
        v, i = _top_rows(_after(scores[half], done), None, PEER_TOPK)
        done = v[PEER_TOPK - 1:PEER_TOPK]
        ts.append(v)
        ti.append(i)
    half_k = PEER_TOPK // 2
    cand_s = [ts[0][0:1] + ts[1]]
    cand_i = [ti[0][0:1] * N_KEYS + ti[1]]
    for a in range(1, half_k):
        cand_s.append(ts[0][a:a + 1] + ts[1][:half_k])
        cand_i.append(ti[0][a:a + 1] * N_KEYS + ti[1][:half_k])
    cand_s.append(ts[0][half_k:] + ts[1][0:1])
    cand_i.append(ti[0][half_k:] * N_KEYS + ti[1][0:1])
    best_s, best_i = _top_rows(jnp.concatenate(cand_s, axis=0), jnp.concatenate(cand_i, axis=0), PEER_TOPK)
    e = jnp.exp(best_s - best_s[0:1])
    return best_i * ROW_WORDS, e / jnp.sum(e, axis=0, keepdims=True)


def pack_expert_table(t):
    e, d = t.shape
    tb = t.astype(BF16).reshape(e, ROW_WORDS, 2, LANES)
    words = lax.bitcast_convert_type(jnp.swapaxes(tb, 2, 3), I32)
    return words.reshape(e * ROW_WORDS, LANES)


def _gather_rows(row_ref, tbl_ref, stage_ref, n):
    offs = row_ref.at[n]
    for j in range(PEER_SLOTS):
        start = pl.multiple_of(offs[j], ROW_WORDS)
        stage_ref[j * ROW_WORDS:(j + 1) * ROW_WORDS, :] = tbl_ref[pl.ds(start, ROW_WORDS), :]


def _chunk_mask(d):
    q = lax.broadcasted_iota(I32, (SUBLANES, d), 0)
    c = lax.broadcasted_iota(I32, (SUBLANES, d), 1)
    return (c % SUBLANES == q).astype(F32)


def _for_token_groups(tb, idx_ref, tbl_ref, stage_refs, consume):
    group = len(stage_refs)

    def body(i, carry):
        base = i * group
        _gather_rows(idx_ref, tbl_ref, stage_refs[0], base)
        for t in range(group):
            if t + 1 < group:
                _gather_rows(idx_ref, tbl_ref, stage_refs[t + 1], base + t + 1)
            consume(base + t, stage_refs[t])
        return carry

    lax.fori_loop(0, _exact_div(tb, group), body, 0)


def _peer_route_act_kernel(q_ref, x_ref, keys_ref, tbl_ref, rep_t_ref, rep_ref, off_ref, w_ref,
                           off_smem, off_vmem, gate_scr, off_t_scr, gate_t_scr, r_ref, sem, *stage_refs, tb):
    d = PEER_SLOTS * SUBLANES
    mask = _chunk_mask(d)
    group = len(stage_refs)
    assert tb == N_HEADS * group
    hand_over = pltpu.make_async_copy(off_vmem, off_smem, sem)

    @pl.when(pl.program_id(0) == 0)
    def _():
        gate_scr[...] = jnp.zeros_like(gate_scr)
        off_vmem[...] = jnp.zeros_like(off_vmem)
        hand_over.start()
        hand_over.wait()

    def consume(n, stage_ref):
        rows = pltpu.bitcast(stage_ref[...], BF16)
        out = _dot_nt(x_ref[n].astype(BF16), rows)
        r_ref[n] = out * mask

    def body(k, carry):
        heads = [ROUTE_PAIR * k + sub for sub in range(ROUTE_PAIR)]
        scores = [_route_scores(keys_ref, [q_ref[2 * h], q_ref[2 * h + 1]]) for h in heads]
        for sub in range(ROUTE_PAIR):
            base = (ROUTE_PAIR * k + sub) * group
            _gather_rows(off_smem, tbl_ref, stage_refs[0], base)
            for t in range(group):
                if t + 1 < group:
                    _gather_rows(off_smem, tbl_ref, stage_refs[t + 1], base + t + 1)
                consume(base + t, stage_refs[t])
        done = None
        for h, sc in zip(heads, scores):
            offs, gates = _route_select(sc, done)
            done = gates[PEER_TOPK - 1:PEER_TOPK]
            slot0 = pl.multiple_of(h * PEER_TOPK, PEER_TOPK)
            off_t_scr[pl.ds(slot0, PEER_TOPK), :] = offs
            gate_t_scr[pl.ds(slot0, PEER_TOPK), :] = gates
        return carry

    lax.fori_loop(0, N_HEADS // ROUTE_PAIR, body, 0)
    part = _split_dot(r_ref[...].reshape(tb * SUBLANES, d), rep_t_ref[...])
    act = jnp.sum(part.reshape(tb, SUBLANES, PEER_SLOTS), axis=1)
    w = gate_scr[...] * _gelu_exact(act)
    w_ref[...] = _dot(w.astype(BF16), rep_ref[...])
    offs_tok = off_t_scr[...].T
    off_ref[...] = offs_tok
    off_vmem[...] = offs_tok
    gate_scr[...] = gate_t_scr[...].T
    hand_over.start()
    hand_over.wait()


def _peer_mix_kernel(idx_ref, w_ref, res_ref, tbl_ref, o_ref, *stage_refs, tb):
    d = PEER_SLOTS * SUBLANES
    mask = _chunk_mask(LANES)

    def consume(n, stage_ref):
        rows = pltpu.bitcast(stage_ref[...], BF16)
        wn = w_ref[n]
        wsel = jnp.concatenate(
            [jnp.broadcast_to(wn[t:t + 1, :], (SUBLANES, LANES)) * mask for t in range(d // LANES)], axis=1)
        o_ref[n] = res_ref[n] + _dot(wsel.astype(BF16), rows)

    _for_token_groups(tb, idx_ref, tbl_ref, stage_refs, consume)


def peer_experts(q3, sub_keys, xn, res, u_tbl, v_tbl):
    n, dm = xn.shape
    d = PEER_SLOTS * SUBLANES
    chunks = dm // LANES
    slot_of_col = jnp.arange(d, dtype=I32) // SUBLANES
    rep = (slot_of_col[None, :] == jnp.arange(PEER_SLOTS, dtype=I32)[:, None]).astype(BF16)
    tbl_spec = pl.BlockSpec(u_tbl.shape, lambda i: (0, 0), pipeline_mode=pl.Buffered(1))
    stages = [pltpu.VMEM((PEER_SLOTS * ROW_WORDS, LANES), I32)] * PEER_TOKEN_GROUP
    tb = N_HEADS * PEER_TOKEN_GROUP
    nb = _exact_div(n, tb)
    routed = lambda i: jnp.minimum(i, nb - 1)
    active = lambda i: jnp.maximum(i - 1, 0)
    idx, w = pl.pallas_call(
        functools.partial(_peer_route_act_kernel, tb=tb),
        grid=(nb + 1,),
        in_specs=[pl.BlockSpec((2 * N_HEADS, tb, N_KEYS), lambda i: (0, routed(i), 0)),
                  pl.BlockSpec((tb, chunks, LANES), lambda i: (active(i), 0, 0)),
                  pl.BlockSpec((2, N_KEYS, N_KEYS), lambda i: (0, 0, 0)),
                  tbl_spec,
                  pl.BlockSpec((d, PEER_SLOTS), lambda i: (0, 0)),
                  pl.BlockSpec((PEER_SLOTS, d), lambda i: (0, 0))],
        out_specs=[pl.BlockSpec((tb, PEER_SLOTS), lambda i: (routed(i), 0)),
                   pl.BlockSpec((tb, d), lambda i: (active(i), 0))],
        out_shape=[jax.ShapeDtypeStruct((n, PEER_SLOTS), I32),
                   jax.ShapeDtypeStruct((n, d), F32)],
        scratch_shapes=[pltpu.SMEM((tb, PEER_SLOTS), I32),
                        pltpu.VMEM((tb, PEER_SLOTS), I32),
                        pltpu.VMEM((tb, PEER_SLOTS), F32),
                        pltpu.VMEM((PEER_SLOTS, tb), I32),
                        pltpu.VMEM((PEER_SLOTS, tb), F32),
                        pltpu.VMEM((tb, SUBLANES, d), F32),
                        pltpu.SemaphoreType.DMA] + stages,
        compiler_params=_cparams(("arbitrary",)),
        name="peer_route_act",
    )(q3, xn.reshape(n, chunks, LANES), sub_keys, u_tbl, rep.T, rep)
    tb = 64
    smem_idx = pl.BlockSpec((tb, PEER_SLOTS), lambda i: (i, 0), memory_space=pltpu.SMEM)
    out = pl.pallas_call(
        functools.partial(_peer_mix_kernel, tb=tb),
        grid=(_exact_div(n, tb),),
        in_specs=[smem_idx,
                  pl.BlockSpec((tb, d // LANES, LANES), lambda i: (i, 0, 0)),
                  pl.BlockSpec((tb, chunks, LANES), lambda i: (i, 0, 0)),
                  tbl_spec],
        out_specs=pl.BlockSpec((tb, chunks, LANES), lambda i: (i, 0, 0)),
        out_shape=jax.ShapeDtypeStruct((n, chunks, LANES), F32),
        scratch_shapes=stages,
        compiler_params=_cparams(("arbitrary",)),
        name="peer_mix",
    )(idx, w.reshape(n, d // LANES, LANES), res.reshape(n, chunks, LANES), v_tbl)
    return out.reshape(n, dm)


def peer_ffn(hs, norm_g, w_q, sub_keys, u, v):
    w_q = w_q.astype(BF16)
    u_tbl, v_tbl = pack_expert_table(u), pack_expert_table(v)
    outs = []
    for h in hs:
        q3, xn = norm_matmul(h, norm_g, w_q, with_xn=True, slabs=True)
        outs.append(peer_experts(q3, sub_keys, xn, h, u_tbl, v_tbl))
    return outs


def kernel(x_prompt, x_sample, state_conv, state_ssm, cache_k, cache_v, cache_logf, page_table, meta_tokens,
           norm_mix, norm_ffn, norm_final, gdn_w_in, gdn_conv_w, gdn_a_log, gdn_dt_bias, gdn_out_norm,
           gdn_w_out, fox_w_in, fox_b_f, fox_w_out, peer_w_q, peer_sub_keys, peer_u, peer_v):
    bp, seq, dm = x_prompt.shape
    bs, sn, _ = x_sample.shape
    hd = N_HEADS * HEAD_DIM
    tp = LEAD_PAD + N_META + seq
    first = LEAD_PAD
    n_p = bp * tp
    n_s = bs * sn

    meta = jnp.broadcast_to(meta_tokens[None], (bp, N_META, dm))
    h_p = jnp.concatenate([jnp.zeros((bp, LEAD_PAD, dm), F32), meta, x_prompt], axis=1).reshape(n_p, dm)
    h_s = x_sample.reshape(n_s, dm)

    w_in = gdn_w_in[0]
    wg = jnp.pad(w_in, ((0, 0), (0, 4 * hd + LANES - w_in.shape[1]))).astype(BF16)
    w_out = gdn_w_out[0].astype(BF16)
    z_p2 = norm_matmul(h_p, norm_mix[0], wg)
    z_s2 = norm_matmul(h_s, norm_mix[0], wg)
    zw = z_p2.shape[1]
    z_p = z_p2.reshape(bp, tp, zw)
    z_s = jnp.pad(z_s2.reshape(bs, sn, zw), ((0, 0), (0, SAMPLE_CHUNK - sn), (0, 0)))
    qkv_p, gb_p = gdn_prep(z_p, z_p, gdn_conv_w[0], gdn_a_log[0], gdn_dt_bias[0],
                           tb=128, zero_first=True, t_lo=first, t_hi=tp)
    o_p, ssm_p = gdn_chunks(qkv_p, gb_p, jnp.zeros((bp, N_HEADS, HEAD_DIM, HEAD_DIM), F32), GDN_CHUNK)
    prev_s = jnp.pad(state_conv[0], ((0, 0), (SUBLANES - (CONV_W - 1), 0), (0, 0)))
    qkv_s, gb_s = gdn_prep(z_s, prev_s, gdn_conv_w[0], gdn_a_log[0], gdn_dt_bias[0],
                           tb=SAMPLE_CHUNK, zero_first=False, t_lo=0, t_hi=sn)
    o_s, ssm_s = gdn_chunks(qkv_s, gb_s, state_ssm[0], SAMPLE_CHUNK)
    h_p = gdn_out(o_p.reshape(n_p, hd), z_p2, 3, gdn_out_norm[0], w_out, h_p)
    h_s = gdn_out(o_s[:, :sn].reshape(n_s, hd), z_s2, 3, gdn_out_norm[0], w_out, h_s)
    new_conv_p = z_p[:, tp - (CONV_W - 1):, :3 * hd][None]
    new_conv_s = z_s[:, sn - (CONV_W - 1):sn, :3 * hd][None]
    h_p, h_s = peer_ffn([h_p, h_s], norm_ffn[0], peer_w_q[0], peer_sub_keys[0], peer_u[0], peer_v[0])

    w_in = fox_w_in[0]
    wf = jnp.pad(w_in, ((0, 0), (0, 3 * hd + LANES - w_in.shape[1]))).astype(BF16)
    w_out = fox_w_out[0].astype(BF16)
    z_p2 = norm_matmul(h_p, norm_mix[1], wf)
    zw = z_p2.shape[1]
    z_p = z_p2.reshape(bp, tp, zw)
    z_s = norm_matmul(h_s, norm_mix[1], wf).reshape(bs, sn, zw)
    lf_p, cum_p = logf_cumsum(z_p, fox_b_f[0], (3 * hd) // LANES, tb=128)
    o_p = fox_prompt(z_p, cum_p, tq=384, first_key=first)
    z_s8 = jnp.pad(z_s, ((0, 0), (0, SUBLANES - sn), (0, 0)))
    lf_s, _ = logf_cumsum(z_s8, fox_b_f[0], (3 * hd) // LANES, tb=SUBLANES)
    lf_s = lf_s[:, :sn, :N_HEADS]
    lfn = jnp.pad(lf_s.reshape(bs, 1, sn * N_HEADS), ((0, 0), (0, 0), (0, LANES - sn * N_HEADS)))
    o_s = fox_sample(z_s[..., :3 * hd].reshape(bs, sn, 3 * N_HEADS, HEAD_DIM), lfn,
                     cache_k.reshape(-1, PAGE, N_HEADS, HEAD_DIM), cache_v.reshape(-1, PAGE, N_HEADS, HEAD_DIM),
                     cache_logf.reshape(-1, SUBLANES, LANES), page_table)
    h_p = matmul_residual(o_p.reshape(n_p, hd), w_out, h_p)
    h_s = matmul_residual(o_s.reshape(n_s, hd), w_out, h_s)
    kv_p = z_p[:, first:, hd:3 * hd]
    new_k_p = kv_p[..., :hd].reshape(1, bp, tp - first, N_HEADS, HEAD_DIM)
    new_v_p = kv_p[..., hd:].reshape(1, bp, tp - first, N_HEADS, HEAD_DIM)
    new_logf_p = lf_p[:, first:, :N_HEADS][None]
    new_k_s = z_s[..., hd:2 * hd].reshape(1, bs, sn, N_HEADS, HEAD_DIM)
    new_v_s = z_s[..., 2 * hd:3 * hd].reshape(1, bs, sn, N_HEADS, HEAD_DIM)
    new_logf_s = lf_s[None]
    h_p, h_s = peer_ffn([h_p, h_s], norm_ffn[1], peer_w_q[1], peer_sub_keys[1], peer_u[1], peer_v[1])

    y_prompt = final_norm(h_p.reshape(bp, tp, dm), norm_final, (LEAD_PAD + N_META) // 128, seq)
    y_sample = final_norm(h_s.reshape(1, n_s, dm), norm_final, 0, n_s).reshape(bs, sn, dm)
    return (y_prompt, y_sample, new_conv_p, ssm_p[None], new_k_p, new_v_p, new_logf_p,
            new_conv_s, ssm_s[None], new_k_s, new_v_s, new_logf_s)
```

```python
import functools
import math

import jax
import jax.numpy as jnp
from jax import lax
from jax.experimental import pallas as pl
from jax.experimental.pallas import tpu as pltpu

F32 = jnp.float32
BF16 = jnp.bfloat16
I32 = jnp.int32
HI = lax.Precision.HIGHEST

EPS = 1e-6
LANES = 128
SUBLANES = 8
VMEM_LIMIT = 56 * 1024 * 1024

N_META = 16
LEAD_PAD = 112
HEAD_DIM = 128
N_HEADS = 8
CONV_W = 4
GDN_CHUNK = 64
SAMPLE_CHUNK = 16
INV_BLOCK = 16
PAGE = 128
PEER_TOPK = 16
N_KEYS = 128
PEER_SLOTS = N_HEADS * PEER_TOPK
ROW_WORDS = 4
PEER_TOKEN_GROUP = 16
ROUTE_PAIR = 2
NEG = -1e30
LOG2E = 1.4426950408889634


def _exact_div(n, d):
    assert n % d == 0, (n, d)
    return n // d


def _cparams(sem):
    return pltpu.CompilerParams(dimension_semantics=sem, vmem_limit_bytes=VMEM_LIMIT)


def _dot(a, b, precision=None):
    return jnp.dot(a, b, preferred_element_type=F32, precision=precision)


def _dot_nt(a, b, precision=None):
    return lax.dot_general(a, b, (((1,), (1,)), ((), ())), preferred_element_type=F32, precision=precision)


def _dot_tn(a, b, precision=None):
    return lax.dot_general(a, b, (((0,), (0,)), ((), ())), preferred_element_type=F32, precision=precision)


def _split_dot(a, b01):
    hi = a.astype(BF16)
    lo = (a - hi.astype(F32)).astype(BF16)
    return _dot(hi, b01) + _dot(lo, b01)


def _dot3(a, b):
    a_hi = a.astype(BF16)
    b_hi = b.astype(BF16)
    a_lo = (a - a_hi.astype(F32)).astype(BF16)
    b_lo = (b - b_hi.astype(F32)).astype(BF16)
    return _dot(a_hi, b_hi) + (_dot(a_hi, b_lo) + _dot(a_lo, b_hi))


def _softplus(x):
    return jnp.maximum(x, 0.0) + jnp.log1p(jnp.exp(-jnp.abs(x)))


def _log_sigmoid(x):
    return -_softplus(-x)


def _gelu_exact(x):
    return 0.5 * x * (1.0 + lax.erf(x * (2.0 ** -0.5)))


def _norm_matmul_kernel(x_ref, g_ref, w_ref, o_ref, *xn_ref, slabs):
    x = x_ref[...]
    xn = x * lax.rsqrt(jnp.mean(x * x, axis=-1, keepdims=True) + EPS) * g_ref[...]
    if xn_ref:
        xn_ref[0][...] = xn
    y = _dot(xn.astype(BF16), w_ref[...])
    if slabs:
        for s in range(y.shape[1] // LANES):
            o_ref[s] = y[:, s * LANES:(s + 1) * LANES]
    else:
        o_ref[...] = y


def norm_matmul(x, g, w_bf16, tm=256, with_xn=False, slabs=False):
    n, k = x.shape
    m = w_bf16.shape[1]
    if slabs:
        out_shape = [jax.ShapeDtypeStruct((m // LANES, n, LANES), F32)]
        out_specs = [pl.BlockSpec((m // LANES, tm, LANES), lambda i: (0, i, 0))]
    else:
        out_shape = [jax.ShapeDtypeStruct((n, m), F32)]
        out_specs = [pl.BlockSpec((tm, m), lambda i: (i, 0))]
    if with_xn:
        out_shape.append(jax.ShapeDtypeStruct((n, k), F32))
        out_specs.append(pl.BlockSpec((tm, k), lambda i: (i, 0)))
    res = pl.pallas_call(
        functools.partial(_norm_matmul_kernel, slabs=slabs),
        grid=(_exact_div(n, tm),),
        in_specs=[pl.BlockSpec((tm, k), lambda i: (i, 0)),
                  pl.BlockSpec((1, k), lambda i: (0, 0)),
                  pl.BlockSpec((k, m), lambda i: (0, 0))],
        out_specs=out_specs,
        out_shape=out_shape,
        compiler_params=_cparams(("parallel",)),
        name="norm_matmul",
    )(x, g.reshape(1, k), w_bf16)
    return res if with_xn else res[0]


def _matmul_res_kernel(x_ref, w_ref, r_ref, o_ref):
    o_ref[...] = r_ref[...] + _dot(x_ref[...].astype(BF16), w_ref[...])


def matmul_residual(x, w_bf16, res, tm=256):
    n, k = x.shape
    m = w_bf16.shape[1]
    return pl.pallas_call(
        _matmul_res_kernel,
        grid=(_exact_div(n, tm),),
        in_specs=[pl.BlockSpec((tm, k), lambda i: (i, 0)),
                  pl.BlockSpec((k, m), lambda i: (0, 0)),
                  pl.BlockSpec((tm, m), lambda i: (i, 0))],
        out_specs=pl.BlockSpec((tm, m), lambda i: (i, 0)),
        out_shape=jax.ShapeDtypeStruct((n, m), F32),
        compiler_params=_cparams(("parallel",)),
        name="matmul_residual",
    )(x, w_bf16, res)


def _gdn_out_kernel(o_ref, gate_ref, nw_ref, w_ref, r_ref, y_ref):
    o = o_ref[...]
    gate = gate_ref[...]
    nw = nw_ref[...]
    parts = []
    for h in range(N_HEADS):
        sl = slice(h * HEAD_DIM, (h + 1) * HEAD_DIM)
        oh = o[:, sl]
        oh = oh * lax.rsqrt(jnp.mean(oh * oh, axis=-1, keepdims=True) + EPS) * nw
        parts.append(oh * jax.nn.silu(gate[:, sl]))
    x = jnp.concatenate(parts, axis=-1)
    y_ref[...] = r_ref[...] + _dot(x.astype(BF16), w_ref[...])


def gdn_out(o, z, gate_col_block, out_norm, w_bf16, res, tm=256):
    n, d = o.shape
    return pl.pallas_call(
        _gdn_out_kernel,
        grid=(_exact_div(n, tm),),
        in_specs=[pl.BlockSpec((tm, d), lambda i: (i, 0)),
                  pl.BlockSpec((tm, d), lambda i: (i, gate_col_block)),
                  pl.BlockSpec((1, HEAD_DIM), lambda i: (0, 0)),
                  pl.BlockSpec((d, d), lambda i: (0, 0)),
                  pl.BlockSpec((tm, d), lambda i: (i, 0))],
        out_specs=pl.BlockSpec((tm, d), lambda i: (i, 0)),
        out_shape=jax.ShapeDtypeStruct((n, d), F32),
        compiler_params=_cparams(("parallel",)),
        name="gdn_out",
    )(o, z, out_norm.reshape(1, HEAD_DIM), w_bf16, res)


def _final_norm_kernel(x_ref, g_ref, o_ref):
    x = x_ref[...]
    o_ref[...] = x * lax.rsqrt(jnp.mean(x * x, axis=-1, keepdims=True) + EPS) * g_ref[...]


def final_norm(x3, g, row_block_offset, t_out, tb=128):
    b, _, d = x3.shape
    return pl.pallas_call(
        _final_norm_kernel,
        grid=(b, t_out // tb),
        in_specs=[pl.BlockSpec((None, tb, d), lambda i, j: (i, j + row_block_offset, 0)),
                  pl.BlockSpec((1, d), lambda i, j: (0, 0))],
        out_specs=pl.BlockSpec((None, tb, d), lambda i, j: (i, j, 0)),
        out_shape=jax.ShapeDtypeStruct((b, t_out, d), F32),
        compiler_params=_cparams(("parallel", "parallel")),
        name="final_norm",
    )(x3, g.reshape(1, d))


def _gdn_prep_kernel(prev_ref, cur_ref, ab_ref, convw_ref, par_ref, qkv_ref, gb_ref, *,
                     tb, zero_first, t_lo, t_hi):
    t = pl.program_id(1)
    cur = cur_ref[...]
    prev = prev_ref[...]
    if zero_first:
        prev = jnp.where(t == 0, 0.0, prev)
    cat = jnp.concatenate([prev, cur], axis=0)
    w = convw_ref[...]
    y = None
    for i in range(CONV_W):
        shift = CONV_W - 1 - i
        xi = cur if shift == 0 else pltpu.roll(cat, shift, axis=0)[SUBLANES:]
        term = xi * w[i:i + 1]
        y = term if y is None else y + term
    y = jax.nn.silu(y)
    hd = N_HEADS * HEAD_DIM
    for h in range(2 * N_HEADS):
        sl = slice(h * HEAD_DIM, (h + 1) * HEAD_DIM)
        xh = y[:, sl]
        xh = xh * lax.rsqrt(jnp.sum(xh * xh, axis=-1, keepdims=True) + EPS)
        if h < N_HEADS:
            xh = xh * (HEAD_DIM ** -0.5)
        qkv_ref[:, sl] = xh
    qkv_ref[:, 2 * hd:] = y[:, 2 * hd:]
    ab = ab_ref[...]
    a_log = par_ref[0:1, :]
    dt_bias = par_ref[1:2, :]
    g = -jnp.exp(a_log) * _softplus(ab + dt_bias)
    beta = jax.nn.sigmoid(ab)
    lane = lax.broadcasted_iota(I32, ab.shape, 1)
    pos = t * tb + lax.broadcasted_iota(I32, ab.shape, 0)
    valid = (pos >= t_lo) & (pos < t_hi)
    gb = jnp.where(lane < N_HEADS, g, jnp.where(lane < 2 * N_HEADS, beta, 0.0))
    gb_ref[...] = jnp.where(valid, gb, 0.0)


def gdn_prep(z3, prev_src, conv_w, a_log, dt_bias, tb, zero_first, t_lo, t_hi):
    b, t, _ = z3.shape
    c = 3 * N_HEADS * HEAD_DIM
    ab_block = (4 * N_HEADS * HEAD_DIM) // LANES
    par = jnp.zeros((SUBLANES, LANES), F32)
    par = par.at[0, :N_HEADS].set(a_log).at[1, :N_HEADS].set(dt_bias)
    if zero_first:
        prev_map = lambda i, j: (i, jnp.maximum(j * (tb // SUBLANES) - 1, 0), 0)
    else:
        prev_map = lambda i, j: (i, 0, 0)
    kern = functools.partial(_gdn_prep_kernel, tb=tb, zero_first=zero_first, t_lo=t_lo, t_hi=t_hi)
    return pl.pallas_call(
        kern,
        grid=(b, t // tb),
        in_specs=[pl.BlockSpec((None, SUBLANES, c), prev_map),
                  pl.BlockSpec((None, tb, c), lambda i, j: (i, j, 0)),
                  pl.BlockSpec((None, tb, LANES), lambda i, j: (i, j, ab_block)),
                  pl.BlockSpec((CONV_W, c), lambda i, j: (0, 0)),
                  pl.BlockSpec((SUBLANES, LANES), lambda i, j: (0, 0))],
        out_specs=[pl.BlockSpec((None, tb, c), lambda i, j: (i, j, 0)),
                   pl.BlockSpec((None, tb, LANES), lambda i, j: (i, j, 0))],
        out_shape=[jax.ShapeDtypeStruct((b, t, c), F32),
                   jax.ShapeDtypeStruct((b, t, LANES), F32)],
        compiler_params=_cparams(("parallel", "arbitrary")),
        name="gdn_prep",
    )(prev_src, z3, z3, conv_w, par)


def _unit_lower_inverses(mats, n):
    ii = lax.broadcasted_iota(I32, (n, n), 0)
    jj = lax.broadcasted_iota(I32, (n, n), 1)
    eye = (ii == jj).astype(F32)
    blk = min(INV_BLOCK, n)
    if n > blk:
        same = (ii // blk) == (jj // blk)
        d = [jnp.where(same, a, 0.0) for a in mats]
        rest = [jnp.where(same, 0.0, a) for a in mats]
    else:
        d = mats
    x = [eye - di for di in d]
    p = d
    width = 1
    while 2 * width < blk:
        p = [_dot3(pi, pi) for pi in p]
        x = [xi + _dot3(xi, pi) for xi, pi in zip(x, p)]
        width *= 2
    if n == blk:
        return x
    nm = [_dot3(xi, ri) for xi, ri in zip(x, rest)]
    y = [eye - ni for ni in nm]
    p = nm
    width = 1
    while 2 * width < n // blk:
        p = [_dot3(pi, pi) for pi in p]
        y = [yi + _dot3(yi, pi) for yi, pi in zip(y, p)]
        width *= 2
    return [_dot3(yi, xi) for yi, xi in zip(y, x)]


def _bdot(a, b):
    return _dot(a.astype(BF16), b.astype(BF16))


def _bdot_nt(a, b):
    return _dot_nt(a.astype(BF16), b.astype(BF16))


def _gdn_chunk_kernel(qkv_ref, gb_ref, gbt_ref, s0_ref, o_ref, sfin_ref, state_ref, *, ln):
    c = pl.program_id(1)

    @pl.when(c == 0)
    def _():
        state_ref[...] = s0_ref[...]

    hd = N_HEADS * HEAD_DIM
    heads = range(N_HEADS)
    ii = lax.broadcasted_iota(I32, (ln, ln), 0)
    jj = lax.broadcasted_iota(I32, (ln, ln), 1)
    tri_l = (ii >= jj).astype(F32)
    gb = gb_ref[...]
    gbt = gbt_ref[...]
    gc_cols = _dot(tri_l, gb, HI)
    gc_rows = _dot_nt(gbt, tri_l, HI)
    causal = ii >= jj
    strict = ii > jj
    q = [qkv_ref[:, h * HEAD_DIM:(h + 1) * HEAD_DIM] for h in heads]
    k = [qkv_ref[:, hd + h * HEAD_DIM:hd + (h + 1) * HEAD_DIM] for h in heads]
    v = [qkv_ref[:, 2 * hd + h * HEAD_DIM:2 * hd + (h + 1) * HEAD_DIM] for h in heads]
    s = [state_ref[h] for h in heads]
    beta = [gb[:, N_HEADS + h:N_HEADS + h + 1] for h in heads]
    gc = [gc_cols[:, h:h + 1] for h in heads]
    decay = [jnp.where(causal, jnp.exp(jnp.where(causal, gc[h] - gc_rows[h:h + 1, :], 0.0)), 0.0) for h in heads]
    kb = [k[h] * beta[h] for h in heads]
    a = [jnp.where(strict, _bdot_nt(kb[h], k[h]) * decay[h], 0.0) for h in heads]
    tinv = _unit_lower_inverses(a, ln)
    egc = [jnp.exp(gc[h]) for h in heads]
    u = [_dot3(tinv[h], v[h] * beta[h]) for h in heads]
    w = [_dot3(tinv[h], kb[h] * egc[h]) for h in heads]
    v_new = [u[h] - _bdot(w[h], s[h]) for h in heads]
    attn = [_bdot_nt(q[h], k[h]) * decay[h] for h in heads]
    o = [_bdot(q[h] * egc[h], s[h]) + _bdot(attn[h], v_new[h]) for h in heads]
    g_last = [gc[h][ln - 1:ln, :] for h in heads]
    k_dec = [(k[h] * jnp.exp(g_last[h] - gc[h])).T for h in heads]
    s_new = [s[h] * jnp.exp(g_last[h]) + _bdot(k_dec[h], v_new[h]) for h in heads]
    o_ref[...] = jnp.concatenate(o, axis=-1)
    for h in heads:
        state_ref[h] = s_new[h]

    @pl.when(c == pl.num_programs(1) - 1)
    def _():
        sfin_ref[...] = state_ref[...]


def gdn_chunks(qkv_c, gb, s0, ln):
    b, t, c = qkv_c.shape
    nc = t // ln
    gbt = jnp.swapaxes(gb[..., :2 * N_HEADS].reshape(b, nc, ln, 2 * N_HEADS), 2, 3)
    hd = N_HEADS * HEAD_DIM
    kern = functools.partial(_gdn_chunk_kernel, ln=ln)
    return pl.pallas_call(
        kern,
        grid=(b, nc),
        in_specs=[pl.BlockSpec((None, ln, c), lambda i, j: (i, j, 0)),
                  pl.BlockSpec((None, ln, LANES), lambda i, j: (i, j, 0)),
                  pl.BlockSpec((None, None, 2 * N_HEADS, ln), lambda i, j: (i, j, 0, 0)),
                  pl.BlockSpec((None, N_HEADS, HEAD_DIM, HEAD_DIM), lambda i, j: (i, 0, 0, 0))],
        out_specs=[pl.BlockSpec((None, ln, hd), lambda i, j: (i, j, 0)),
                   pl.BlockSpec((None, N_HEADS, HEAD_DIM, HEAD_DIM), lambda i, j: (i, 0, 0, 0))],
        out_shape=[jax.ShapeDtypeStruct((b, t, hd), F32),
                   jax.ShapeDtypeStruct((b, N_HEADS, HEAD_DIM, HEAD_DIM), F32)],
        scratch_shapes=[pltpu.VMEM((N_HEADS, HEAD_DIM, HEAD_DIM), F32)],
        compiler_params=_cparams(("parallel", "arbitrary")),
        name="gdn_chunks",
    )(qkv_c, gb, gbt, s0)


def _logf_cumsum_kernel(z_ref, bf_ref, lf_ref, cum_ref, carry_ref, *, tb):
    @pl.when(pl.program_id(1) == 0)
    def _():
        carry_ref[...] = jnp.zeros_like(carry_ref)

    lf = _log_sigmoid(z_ref[...] + bf_ref[...])
    lf_ref[...] = lf
    ii = lax.broadcasted_iota(I32, (tb, tb), 0)
    jj = lax.broadcasted_iota(I32, (tb, tb), 1)
    cum = _dot((ii >= jj).astype(F32), lf, HI) + carry_ref[...]
    cum_ref[...] = cum
    carry_ref[...] = cum[tb - 1:tb, :]


def logf_cumsum(z3, b_f, col_block, tb):
    b, t, _ = z3.shape
    bf = jnp.zeros((1, LANES), F32).at[0, :N_HEADS].set(b_f)
    kern = functools.partial(_logf_cumsum_kernel, tb=tb)
    return pl.pallas_call(
        kern,
        grid=(b, t // tb),
        in_specs=[pl.BlockSpec((None, tb, LANES), lambda i, j: (i, j, col_block)),
                  pl.BlockSpec((1, LANES), lambda i, j: (0, 0))],
        out_specs=[pl.BlockSpec((None, tb, LANES), lambda i, j: (i, j, 0)),
                   pl.BlockSpec((None, tb, LANES), lambda i, j: (i, j, 0))],
        out_shape=[jax.ShapeDtypeStruct((b, t, LANES), F32),
                   jax.ShapeDtypeStruct((b, t, LANES), F32)],
        scratch_shapes=[pltpu.VMEM((1, LANES), F32)],
        compiler_params=_cparams(("parallel", "arbitrary")),
        name="logf_cumsum",
    )(z3, bf)


def _fox_prompt_kernel(q_ref, k_ref, vt_ref, ck_ref, o_ref, m_ref, l_ref, acc_ref, *, tq, group, first_key):
    qi = pl.program_id(1)
    ki = pl.program_id(2)

    @pl.when(ki == 0)
    def _():
        m_ref[...] = jnp.full_like(m_ref, NEG)
        l_ref[...] = jnp.zeros_like(l_ref)
        acc_ref[...] = jnp.zeros_like(acc_ref)

    @pl.when(ki <= qi)
    def _():
        ck = ck_ref[...] * LOG2E
        kpos = ki * tq + lax.broadcasted_iota(I32, (tq, tq), 0)
        qpos = qi * tq + lax.broadcasted_iota(I32, (tq, tq), 1)
        mask = (qpos >= kpos) & (kpos >= first_key)
        for g in range(N_HEADS // group):
            heads = range(g * group, (g + 1) * group)
            sl = {h: slice(h * HEAD_DIM, (h + 1) * HEAD_DIM) for h in heads}
            m_old = {h: m_ref[h] for h in heads}
            l_old = {h: l_ref[h] for h in heads}
            acc_old = {h: acc_ref[h] for h in heads}
            s = {h: _bdot_nt(k_ref[:, sl[h]], q_ref[:, sl[h]]) * (HEAD_DIM ** -0.5 * LOG2E) - ck[:, h:h + 1]
                 for h in heads}
            s = {h: jnp.where(mask, s[h], NEG) for h in heads}
            m_new = {h: jnp.maximum(m_old[h], jnp.max(s[h], axis=0, keepdims=True)) for h in heads}
            alpha = {h: jnp.exp2(m_old[h] - m_new[h]) for h in heads}
            p = {h: jnp.exp2(s[h] - m_new[h]) for h in heads}
            l_new = {h: alpha[h] * l_old[h] + jnp.sum(p[h], axis=0, keepdims=True) for h in heads}
            acc_new = {h: alpha[h] * acc_old[h] + _bdot(vt_ref[h], p[h]) for h in heads}
            for h in heads:
                m_ref[h] = m_new[h]
                l_ref[h] = l_new[h]
                acc_ref[h] = acc_new[h]

    @pl.when(ki == qi)
    def _():
        for h in range(N_HEADS):
            o_ref[:, h * HEAD_DIM:(h + 1) * HEAD_DIM] = (acc_ref[h] / l_ref[h]).T


def fox_prompt(z3, cum, tq, first_key, group=4):
    b, t, _ = z3.shape
    hd = N_HEADS * HEAD_DIM
    v_t = jnp.transpose(z3[..., 2 * hd:3 * hd].reshape(b, t, N_HEADS, HEAD_DIM), (0, 2, 3, 1))
    nq = _exact_div(t, tq)
    kern = functools.partial(_fox_prompt_kernel, tq=tq, group=group, first_key=first_key)
    return pl.pallas_call(
        kern,
        grid=(b, nq, nq),
        in_specs=[pl.BlockSpec((None, tq, hd), lambda i, qi, ki: (i, qi, 0)),
                  pl.BlockSpec((None, tq, hd), lambda i, qi, ki: (i, jnp.minimum(ki, qi), 1)),
                  pl.BlockSpec((None, N_HEADS, HEAD_DIM, tq), lambda i, qi, ki: (i, 0, 0, jnp.minimum(ki, qi))),
                  pl.BlockSpec((None, tq, LANES), lambda i, qi, ki: (i, jnp.minimum(ki, qi), 0))],
        out_specs=pl.BlockSpec((None, tq, hd), lambda i, qi, ki: (i, qi, 0)),
        out_shape=jax.ShapeDtypeStruct((b, t, hd), F32),
        scratch_shapes=[pltpu.VMEM((N_HEADS, 1, tq), F32),
                        pltpu.VMEM((N_HEADS, 1, tq), F32),
                        pltpu.VMEM((N_HEADS, HEAD_DIM, tq), F32)],
        compiler_params=_cparams(("parallel", "parallel", "arbitrary")),
        name="fox_prompt",
    )(z3, z3, v_t, cum)


def _strided_lane_cumsum(x, n_pos):
    lane = lax.broadcasted_iota(I32, x.shape, 1)
    sh = N_HEADS
    while sh < n_pos * N_HEADS:
        x = x + jnp.where(lane >= sh, pltpu.roll(x, sh, axis=1), 0.0)
        sh *= 2
    return x


def _page_cumsum_kernel(lf_ref, cum_ref, tot_ref):
    lf = lf_ref[...]
    n_rows = lf.shape[0]
    x = _strided_lane_cumsum(lf, LANES // N_HEADS)
    lane = lax.broadcasted_iota(I32, lf.shape, 1)
    row = lax.broadcasted_iota(I32, lf.shape, 0) % SUBLANES
    tot = jnp.where(lane >= LANES - N_HEADS, x, 0.0)
    sh = N_HEADS
    while sh < LANES:
        tot = tot + pltpu.roll(tot, LANES - sh, axis=1)
        sh *= 2
    incl = tot
    sh = 1
    while sh < SUBLANES:
        incl = incl + jnp.where(row >= sh, pltpu.roll(incl, sh, axis=0), 0.0)
        sh *= 2
    cum_ref[...] = x + (incl - tot)
    page = jnp.where(row == SUBLANES - 1, incl, 0.0)
    sh = 1
    while sh < SUBLANES:
        page = page + pltpu.roll(page, n_rows - sh, axis=0)
        sh *= 2
    tot_ref[...] = page


def page_cumsum(lf_pool, pages_per_step=64):
    n_pages = lf_pool.shape[0]
    rows = pages_per_step * SUBLANES
    flat = lf_pool.reshape(n_pages * SUBLANES, LANES)
    spec = pl.BlockSpec((rows, LANES), lambda i: (i, 0))
    cum, tot = pl.pallas_call(
        _page_cumsum_kernel,
        grid=(_exact_div(n_pages, pages_per_step),),
        in_specs=[spec],
        out_specs=[spec, spec],
        out_shape=[jax.ShapeDtypeStruct(flat.shape, F32)] * 2,
        compiler_params=_cparams(("parallel",)),
        name="page_cumsum",
    )(flat)
    return cum.reshape(lf_pool.shape), tot.reshape(lf_pool.shape)


def _fox_sample_kernel(pt_ref, q_ref, kn_ref, vn_ref, lfn_ref, *rest, sn, pps):
    del pt_ref
    kp_refs, vp_refs = rest[:pps], rest[pps:2 * pps]
    cum_refs, tot_refs = rest[2 * pps:3 * pps], rest[3 * pps:4 * pps]
    o_ref, m_ref, l_ref, acc_ref, carry_ref = rest[4 * pps:]
    p = pl.program_id(1)
    rows = sn * N_HEADS
    keys = PAGE * N_HEADS
    scale = HEAD_DIM ** -0.5

    @pl.when(p == 0)
    def _():
        m_ref[...] = jnp.full_like(m_ref, NEG)
        l_ref[...] = jnp.zeros_like(l_ref)
        acc_ref[...] = jnp.zeros_like(acc_ref)
        carry_ref[...] = jnp.zeros_like(carry_ref)

    row_head = lax.broadcasted_iota(I32, (rows, LANES), 0) % N_HEADS
    lane_head = lax.broadcasted_iota(I32, (rows, LANES), 1) % N_HEADS
    same_head = row_head == lane_head

    def update(scores, vals):
        m_old = m_ref[...]
        m_new = m_old
        for sc in scores:
            m_new = jnp.maximum(m_new, jnp.max(sc, axis=-1, keepdims=True))
        alpha = jnp.exp(m_old - m_new)
        l_new = alpha * l_ref[...]
        acc = alpha * acc_ref[...]
        for sc, vl in zip(scores, vals):
            pr = jnp.exp(sc - m_new)
            l_new = l_new + jnp.sum(pr, axis=-1, keepdims=True)
            acc = acc + _dot(pr.astype(BF16), vl)
        l_ref[...] = l_new
        acc_ref[...] = acc
        m_ref[...] = m_new

    q = q_ref[...].reshape(rows, HEAD_DIM).astype(BF16)
    carry = carry_ref[...]
    scores, vals = [], []
    for r in range(pps):
        cum = cum_refs[r][...] + carry
        carry = carry + tot_refs[r][...]
        k2 = kp_refs[r][...].reshape(keys, HEAD_DIM).astype(BF16)
        sc = _dot_nt(q, k2) * scale
        tiles = []
        for t in range(keys // LANES):
            tile = sc[:, t * LANES:(t + 1) * LANES] - cum[t:t + 1, :]
            tiles.append(jnp.where(same_head, tile, NEG))
        scores.append(jnp.concatenate(tiles, axis=1))
        vals.append(vp_refs[r][...].reshape(keys, HEAD_DIM).astype(BF16))
    update(scores, vals)
    carry_ref[...] = carry

    @pl.when(p == pl.num_programs(1) - 1)
    def _():
        pad = jnp.zeros((LANES - rows, HEAD_DIM), F32)
        kn = jnp.concatenate([kn_ref[...].reshape(rows, HEAD_DIM), pad], axis=0).astype(BF16)
        vn = jnp.concatenate([vn_ref[...].reshape(rows, HEAD_DIM), pad], axis=0).astype(BF16)
        cnew = _strided_lane_cumsum(lfn_ref[...], sn) + carry_ref[0:1, :]
        scn = _dot_nt(q, kn) * scale - cnew
        q_pos = lax.broadcasted_iota(I32, (rows, LANES), 0) // N_HEADS
        k_pos = lax.broadcasted_iota(I32, (rows, LANES), 1) // N_HEADS
        update([jnp.where(same_head & (k_pos <= q_pos), scn, NEG)], [vn])
        o_ref[...] = (acc_ref[...] / l_ref[...]).reshape(sn, N_HEADS, HEAD_DIM)


def fox_sample(qkv_s, lfn, k_pool, v_pool, cum_pool, tot_pool, page_table, pps=8):
    b, sn = qkv_s.shape[:2]
    n_pages = page_table.shape[1]
    rows = sn * N_HEADS
    kern = functools.partial(_fox_sample_kernel, sn=sn, pps=pps)
    page_map = lambda r: (lambda i, p, pt: (pt[i, p * pps + r], 0, 0, 0))
    lf_map = lambda r: (lambda i, p, pt: (pt[i, p * pps + r], 0, 0))
    new_spec = lambda part: pl.BlockSpec((None, sn, N_HEADS, HEAD_DIM), lambda i, p, pt: (i, 0, part, 0))
    grid_spec = pltpu.PrefetchScalarGridSpec(
        num_scalar_prefetch=1,
        grid=(b, _exact_div(n_pages, pps)),
        in_specs=[new_spec(0), new_spec(1), new_spec(2),
                  pl.BlockSpec((None, 1, LANES), lambda i, p, pt: (i, 0, 0))]
                 + [pl.BlockSpec((None, PAGE, N_HEADS, HEAD_DIM), page_map(r)) for r in range(pps)]
                 + [pl.BlockSpec((None, PAGE, N_HEADS, HEAD_DIM), page_map(r)) for r in range(pps)]
                 + [pl.BlockSpec((None, SUBLANES, LANES), lf_map(r)) for r in range(pps)]
                 + [pl.BlockSpec((None, SUBLANES, LANES), lf_map(r)) for r in range(pps)],
        out_specs=pl.BlockSpec((None, sn, N_HEADS, HEAD_DIM), lambda i, p, pt: (i, 0, 0, 0)),
        scratch_shapes=[pltpu.VMEM((rows, 1), F32),
                        pltpu.VMEM((rows, 1), F32),
                        pltpu.VMEM((rows, HEAD_DIM), F32),
                        pltpu.VMEM((SUBLANES, LANES), F32)],
    )
    return pl.pallas_call(
        kern,
        grid_spec=grid_spec,
        out_shape=jax.ShapeDtypeStruct((b, sn, N_HEADS, HEAD_DIM), F32),
        compiler_params=_cparams(("parallel", "arbitrary")),
        name="fox_sample",
    )(page_table, qkv_s, qkv_s, qkv_s, lfn, *([k_pool] * pps), *([v_pool] * pps),
      *([cum_pool] * pps), *([tot_pool] * pps))


def _top_rows(x, payload, k):
    r = x.shape[0]
    row = lax.broadcasted_iota(I32, x.shape, 0)
    vals, picks = [], []
    for _ in range(k):
        m = jnp.max(x, axis=0, keepdims=True)
        pos = jnp.min(jnp.where(x == m, row, r), axis=0, keepdims=True)
        sel = row == pos
        vals.append(m)
        picks.append(pos if payload is None else jnp.max(jnp.where(sel, payload, -1), axis=0, keepdims=True))
        x = jnp.where(sel, -jnp.inf, x)
    return jnp.concatenate(vals, axis=0), jnp.concatenate(picks, axis=0)


def _route_scores(keys_ref, q_halves):
    return [_dot_nt(keys_ref[half], q_halves[half], HI) for half in range(2)]


def _after(x, done):
    return x if done is None else jnp.where(done == done, x, -jnp.inf)


def _route_select(scores, done=None):
    ts, ti = [], []
    for half in range(2):
        v, i = _top_rows(_after(scores[half], done), None, PEER_TOPK)
        done = v[PEER_TOPK - 1:PEER_TOPK]
        ts.append(v)
        ti.append(i)
    half_k = PEER_TOPK // 2
    cand_s = [ts[0][0:1] + ts[1]]
    cand_i = [ti[0][0:1] * N_KEYS + ti[1]]
    for a in range(1, half_k):
        cand_s.append(ts[0][a:a + 1] + ts[1][:half_k])
        cand_i.append(ti[0][a:a + 1] * N_KEYS + ti[1][:half_k])
    cand_s.append(ts[0][half_k:] + ts[1][0:1])
    cand_i.append(ti[0][half_k:] * N_KEYS + ti[1][0:1])
    best_s, best_i = _top_rows(jnp.concatenate(cand_s, axis=0), jnp.concatenate(cand_i, axis=0), PEER_TOPK)
    e = jnp.exp(best_s - best_s[0:1])
    return best_i * ROW_WORDS, e / jnp.sum(e, axis=0, keepdims=True)


def pack_expert_table(t):
    e, d = t.shape
    tb = t.astype(BF16).reshape(e, ROW_WORDS, 2, LANES)
    words = lax.bitcast_convert_type(jnp.swapaxes(tb, 2, 3), I32)
    return words.reshape(e * ROW_WORDS, LANES)


def _gather_rows(row_ref, tbl_ref, stage_ref, n):
    offs = row_ref.at[n]
    for j in range(PEER_SLOTS):
        start = pl.multiple_of(offs[j], ROW_WORDS)
        stage_ref[j * ROW_WORDS:(j + 1) * ROW_WORDS, :] = tbl_ref[pl.ds(start, ROW_WORDS), :]


def _chunk_mask(d):
    q = lax.broadcasted_iota(I32, (SUBLANES, d), 0)
    c = lax.broadcasted_iota(I32, (SUBLANES, d), 1)
    return (c % SUBLANES == q).astype(F32)


def _for_token_groups(tb, idx_ref, tbl_ref, stage_refs, consume):
    group = len(stage_refs)
    for i in range(_exact_div(tb, group)):
        base = i * group
        _gather_rows(idx_ref, tbl_ref, stage_refs[0], base)
        for t in range(group):
            if t + 1 < group:
                _gather_rows(idx_ref, tbl_ref, stage_refs[t + 1], base + t + 1)
            consume(base + t, stage_refs[t])


def _peer_route_act_kernel(q_ref, x_ref, keys_ref, tbl_ref, rep_t_ref, rep_ref, off_ref, w_ref,
                           off_smem, off_vmem, gate_scr, off_t_scr, gate_t_scr, r_ref, sem, *stage_refs, tb):
    d = PEER_SLOTS * SUBLANES
    mask = _chunk_mask(d)
    group = len(stage_refs)
    assert tb == N_HEADS * group
    hand_over = pltpu.make_async_copy(off_vmem, off_smem, sem)

    @pl.when(pl.program_id(0) == 0)
    def _():
        gate_scr[...] = jnp.zeros_like(gate_scr)
        off_vmem[...] = jnp.zeros_like(off_vmem)
        hand_over.start()
        hand_over.wait()

    def consume(n, stage_ref):
        rows = pltpu.bitcast(stage_ref[...], BF16)
        x_row = x_ref[pl.ds(n, 1), :]
        x_chunks = jnp.concatenate([x_row[:, q * LANES:(q + 1) * LANES] for q in range(SUBLANES)], axis=0)
        out = _dot_nt(x_chunks.astype(BF16), rows)
        r_ref[n] = out * mask

    def body(k, carry):
        heads = [ROUTE_PAIR * k + sub for sub in range(ROUTE_PAIR)]
        scores = [_route_scores(keys_ref, [q_ref[2 * h], q_ref[2 * h + 1]]) for h in heads]
        for sub in range(ROUTE_PAIR):
            base = (ROUTE_PAIR * k + sub) * group
            _gather_rows(off_smem, tbl_ref, stage_refs[0], base)
            for t in range(group):
                if t + 1 < group:
                    _gather_rows(off_smem, tbl_ref, stage_refs[t + 1], base + t + 1)
                consume(base + t, stage_refs[t])
        done = None
        for h, sc in zip(heads, scores):
            offs, gates = _route_select(sc, done)
            done = gates[PEER_TOPK - 1:PEER_TOPK]
            slot0 = pl.multiple_of(h * PEER_TOPK, PEER_TOPK)
            off_t_scr[pl.ds(slot0, PEER_TOPK), :] = offs
            gate_t_scr[pl.ds(slot0, PEER_TOPK), :] = gates
        return carry

    lax.fori_loop(0, N_HEADS // ROUTE_PAIR, body, 0)
    part = _split_dot(r_ref[...].reshape(tb * SUBLANES, d), rep_t_ref[...])
    act = jnp.sum(part.reshape(tb, SUBLANES, PEER_SLOTS), axis=1)
    w = gate_scr[...] * _gelu_exact(act)
    w_ref[...] = _dot(w.astype(BF16), rep_ref[...])
    offs_tok = off_t_scr[...].T
    off_ref[...] = offs_tok
    off_vmem[...] = offs_tok
    gate_scr[...] = gate_t_scr[...].T
    hand_over.start()
    hand_over.wait()


def _peer_mix_kernel(idx_ref, w_ref, res_ref, tbl_ref, o_ref, *stage_refs, tb):
    d = PEER_SLOTS * SUBLANES
    mask = _chunk_mask(LANES)

    def consume(n, stage_ref):
        rows = pltpu.bitcast(stage_ref[...], BF16)
        w_row = w_ref[n:n + 1, :]
        wsel = jnp.concatenate(
            [jnp.broadcast_to(w_row[:, t * LANES:(t + 1) * LANES], (SUBLANES, LANES)) * mask
             for t in range(d // LANES)], axis=1)
        y = _dot(wsel.astype(BF16), rows)
        y_row = jnp.concatenate([y[q:q + 1, :] for q in range(SUBLANES)], axis=1)
        o_ref[n:n + 1, :] = res_ref[n:n + 1, :] + y_row

    _for_token_groups(tb, idx_ref, tbl_ref, stage_refs, consume)


def peer_experts(q3, sub_keys, xn, res, u_tbl, v_tbl):
    n, dm = xn.shape
    d = PEER_SLOTS * SUBLANES
    slot_of_col = jnp.arange(d, dtype=I32) // SUBLANES
    rep = (slot_of_col[None, :] == jnp.arange(PEER_SLOTS, dtype=I32)[:, None]).astype(BF16)
    tbl_spec = pl.BlockSpec(u_tbl.shape, lambda i: (0, 0), pipeline_mode=pl.Buffered(1))
    stages = [pltpu.VMEM((PEER_SLOTS * ROW_WORDS, LANES), I32)] * PEER_TOKEN_GROUP
    tb = N_HEADS * PEER_TOKEN_GROUP
    nb = _exact_div(n, tb)
    routed = lambda i: jnp.minimum(i, nb - 1)
    active = lambda i: jnp.maximum(i - 1, 0)
    idx, w = pl.pallas_call(
        functools.partial(_peer_route_act_kernel, tb=tb),
        grid=(nb + 1,),
        in_specs=[pl.BlockSpec((2 * N_HEADS, tb, N_KEYS), lambda i: (0, routed(i), 0)),
                  pl.BlockSpec((tb, dm), lambda i: (active(i), 0)),
                  pl.BlockSpec((2, N_KEYS, N_KEYS), lambda i: (0, 0, 0)),
                  tbl_spec,
                  pl.BlockSpec((d, PEER_SLOTS), lambda i: (0, 0)),
                  pl.BlockSpec((PEER_SLOTS, d), lambda i: (0, 0))],
        out_specs=[pl.BlockSpec((tb, PEER_SLOTS), lambda i: (routed(i), 0)),
                   pl.BlockSpec((tb, d), lambda i: (active(i), 0))],
        out_shape=[jax.ShapeDtypeStruct((n, PEER_SLOTS), I32),
                   jax.ShapeDtypeStruct((n, d), F32)],
        scratch_shapes=[pltpu.SMEM((tb, PEER_SLOTS), I32),
                        pltpu.VMEM((tb, PEER_SLOTS), I32),
                        pltpu.VMEM((tb, PEER_SLOTS), F32),
                        pltpu.VMEM((PEER_SLOTS, tb), I32),
                        pltpu.VMEM((PEER_SLOTS, tb), F32),
                        pltpu.VMEM((tb, SUBLANES, d), F32),
                        pltpu.SemaphoreType.DMA] + stages,
        compiler_params=_cparams(("arbitrary",)),
        name="peer_route_act",
    )(q3, xn, sub_keys, u_tbl, rep.T, rep)
    tb = 64
    smem_idx = pl.BlockSpec((tb, PEER_SLOTS), lambda i: (i, 0), memory_space=pltpu.SMEM,
                            pipeline_mode=pl.Buffered(1))
    return pl.pallas_call(
        functools.partial(_peer_mix_kernel, tb=tb),
        grid=(_exact_div(n, tb),),
        in_specs=[smem_idx,
                  pl.BlockSpec((tb, d), lambda i: (i, 0)),
                  pl.BlockSpec((tb, dm), lambda i: (i, 0)),
                  tbl_spec],
        out_specs=pl.BlockSpec((tb, dm), lambda i: (i, 0)),
        out_shape=jax.ShapeDtypeStruct((n, dm), F32),
        scratch_shapes=stages,
        compiler_params=_cparams(("arbitrary",)),
        name="peer_mix",
    )(idx, w, res, v_tbl)


def peer_ffn(hs, norm_g, w_q, sub_keys, u, v):
    w_q = w_q.astype(BF16)
    u_tbl, v_tbl = pack_expert_table(u), pack_expert_table(v)
    outs = []
    for h in hs:
        q3, xn = norm_matmul(h, norm_g, w_q, with_xn=True, slabs=True)
        outs.append(peer_experts(q3, sub_keys, xn, h, u_tbl, v_tbl))
    return outs


def kernel(x_prompt, x_sample, state_conv, state_ssm, cache_k, cache_v, cache_logf, page_table, meta_tokens,
           norm_mix, norm_ffn, norm_final, gdn_w_in, gdn_conv_w, gdn_a_log, gdn_dt_bias, gdn_out_norm,
           gdn_w_out, fox_w_in, fox_b_f, fox_w_out, peer_w_q, peer_sub_keys, peer_u, peer_v):
    bp, seq, dm = x_prompt.shape
    bs, sn, _ = x_sample.shape
    hd = N_HEADS * HEAD_DIM
    tp = LEAD_PAD + N_META + seq
    first = LEAD_PAD
    n_p = bp * tp
    n_s = bs * sn

    meta = jnp.broadcast_to(meta_tokens[None], (bp, N_META, dm))
    h_p = jnp.concatenate([jnp.zeros((bp, LEAD_PAD, dm), F32), meta, x_prompt], axis=1).reshape(n_p, dm)
    h_s = x_sample.reshape(n_s, dm)

    w_in = gdn_w_in[0]
    wg = jnp.pad(w_in, ((0, 0), (0, 4 * hd + LANES - w_in.shape[1]))).astype(BF16)
    w_out = gdn_w_out[0].astype(BF16)
    z_p2 = norm_matmul(h_p, norm_mix[0], wg)
    z_s2 = norm_matmul(h_s, norm_mix[0], wg)
    zw = z_p2.shape[1]
    z_p = z_p2.reshape(bp, tp, zw)
    z_s = jnp.pad(z_s2.reshape(bs, sn, zw), ((0, 0), (0, SAMPLE_CHUNK - sn), (0, 0)))
    qkv_p, gb_p = gdn_prep(z_p, z_p, gdn_conv_w[0], gdn_a_log[0], gdn_dt_bias[0],
                           tb=128, zero_first=True, t_lo=first, t_hi=tp)
    o_p, ssm_p = gdn_chunks(qkv_p, gb_p, jnp.zeros((bp, N_HEADS, HEAD_DIM, HEAD_DIM), F32), GDN_CHUNK)
    prev_s = jnp.pad(state_conv[0], ((0, 0), (SUBLANES - (CONV_W - 1), 0), (0, 0)))
    qkv_s, gb_s = gdn_prep(z_s, prev_s, gdn_conv_w[0], gdn_a_log[0], gdn_dt_bias[0],
                           tb=SAMPLE_CHUNK, zero_first=False, t_lo=0, t_hi=sn)
    o_s, ssm_s = gdn_chunks(qkv_s, gb_s, state_ssm[0], SAMPLE_CHUNK)
    h_p = gdn_out(o_p.reshape(n_p, hd), z_p2, 3, gdn_out_norm[0], w_out, h_p)
    h_s = gdn_out(o_s[:, :sn].reshape(n_s, hd), z_s2, 3, gdn_out_norm[0], w_out, h_s)
    new_conv_p = z_p[:, tp - (CONV_W - 1):, :3 * hd][None]
    new_conv_s = z_s[:, sn - (CONV_W - 1):sn, :3 * hd][None]
    h_p, h_s = peer_ffn([h_p, h_s], norm_ffn[0], peer_w_q[0], peer_sub_keys[0], peer_u[0], peer_v[0])

    w_in = fox_w_in[0]
    wf = jnp.pad(w_in, ((0, 0), (0, 3 * hd + LANES - w_in.shape[1]))).astype(BF16)
    w_out = fox_w_out[0].astype(BF16)
    z_p2 = norm_matmul(h_p, norm_mix[1], wf)
    zw = z_p2.shape[1]
    z_p = z_p2.reshape(bp, tp, zw)
    z_s = norm_matmul(h_s, norm_mix[1], wf).reshape(bs, sn, zw)
    lf_p, cum_p = logf_cumsum(z_p, fox_b_f[0], (3 * hd) // LANES, tb=128)
    o_p = fox_prompt(z_p, cum_p, tq=384, first_key=first)
    z_s8 = jnp.pad(z_s, ((0, 0), (0, SUBLANES - sn), (0, 0)))
    lf_s, _ = logf_cumsum(z_s8, fox_b_f[0], (3 * hd) // LANES, tb=SUBLANES)
    lf_s = lf_s[:, :sn, :N_HEADS]
    lfn = jnp.pad(lf_s.reshape(bs, 1, sn * N_HEADS), ((0, 0), (0, 0), (0, LANES - sn * N_HEADS)))
    cum_pool, tot_pool = page_cumsum(cache_logf.reshape(-1, SUBLANES, LANES))
    o_s = fox_sample(z_s[..., :3 * hd].reshape(bs, sn, 3 * N_HEADS, HEAD_DIM), lfn,
                     cache_k.reshape(-1, PAGE, N_HEADS, HEAD_DIM), cache_v.reshape(-1, PAGE, N_HEADS, HEAD_DIM),
                     cum_pool, tot_pool, page_table)
    h_p = matmul_residual(o_p.reshape(n_p, hd), w_out, h_p)
    h_s = matmul_residual(o_s.reshape(n_s, hd), w_out, h_s)
    kv_p = z_p[:, first:, hd:3 * hd]
    new_k_p = kv_p[..., :hd].reshape(1, bp, tp - first, N_HEADS, HEAD_DIM)
    new_v_p = kv_p[..., hd:].reshape(1, bp, tp - first, N_HEADS, HEAD_DIM)
    new_logf_p = lf_p[:, first:, :N_HEADS][None]
    new_k_s = z_s[..., hd:2 * hd].reshape(1, bs, sn, N_HEADS, HEAD_DIM)
    new_v_s = z_s[..., 2 * hd:3 * hd].reshape(1, bs, sn, N_HEADS, HEAD_DIM)
    new_logf_s = lf_s[None]
    h_p, h_s = peer_ffn([h_p, h_s], norm_ffn[1], peer_w_q[1], peer_sub_keys[1], peer_u[1], peer_v[1])

    y_prompt = final_norm(h_p.reshape(bp, tp, dm), norm_final, (LEAD_PAD + N_META) // 128, seq)
    y_sample = final_norm(h_s.reshape(1, n_s, dm), norm_final, 0, n_s).reshape(bs, sn, dm)
    return (y_prompt, y_sample, new_conv_p, ssm_p[None], new_k_p, new_v_p, new_logf_p,
            new_conv_s, ssm_s[None], new_k_s, new_v_s, new_logf_s)
```
